```python
import math
import jax
import jax.numpy as jnp
from jax import lax
import numpy as np

D_MODEL = 1024
BATCH = 8
SEQ = 4096
DEPTH = 1

SG_GROUPS = 8
SG_GROUP_DIM = 128
SG_WIDTH = SG_GROUPS * SG_GROUP_DIM
SG_CHUNK = 128
MOBA_HEADS = 8
MOBA_HEAD_DIM = 128
MOBA_WIDTH = MOBA_HEADS * MOBA_HEAD_DIM
MOBA_BLOCK = 256
MOBA_TOPK = 3
MOBA_QCHUNK = 16
REL_BUCKETS = 32
REL_MAX_DIST = 128
D_FF = 2816
MACARON_WEIGHT = 0.5
IN_WIDTH = 2 * SG_WIDTH + 3 * MOBA_WIDTH + 2 * D_MODEL
DEEPNORM_ALPHA = (2.0 * DEPTH) ** 0.25
DEEPNORM_BETA = (8.0 * DEPTH) ** -0.25
LN_EPS = 1e-5
NEG_INF = -1e30

kernel_name = 'hybrid_gmlp_moba_macaron_deepnorm'


def layer_norm(x, g, b):
    xf = x.astype(jnp.float32)
    mu = jnp.mean(xf, axis=-1, keepdims=True)
    var = jnp.mean(jnp.square(xf - mu), axis=-1, keepdims=True)
    y = (xf - mu) * lax.rsqrt(var + LN_EPS) * g.astype(jnp.float32) + b.astype(jnp.float32)
    return y.astype(x.dtype)


def swiglu_ffn(x, w_gate, w_up, w_down):
    return (jax.nn.silu(x @ w_gate) * (x @ w_up)) @ w_down


def rel_bucket(dist):
    n = jnp.maximum(dist, 0)
    max_exact = REL_BUCKETS // 2
    nf = jnp.maximum(n, 1).astype(jnp.float32)
    large = max_exact + (jnp.log(nf / max_exact) / math.log(REL_MAX_DIST / max_exact)
                         * (REL_BUCKETS - max_exact)).astype(jnp.int32)
    large = jnp.minimum(large, REL_BUCKETS - 1)
    return jnp.where(n < max_exact, n, large)


def spatial_gating(u, v, ln_g, ln_b, w_s, b_s):
    bsz, seq, _ = v.shape
    n_chunks = seq // SG_CHUNK
    v = layer_norm(v, ln_g, ln_b)
    vc = v.reshape(bsz, n_chunks, SG_CHUNK, SG_GROUPS, SG_GROUP_DIM)
    causal = jnp.tril(jnp.ones((SG_CHUNK, SG_CHUNK), dtype=bool))
    w = jnp.where(causal[None], w_s, jnp.zeros_like(w_s))
    mixed = jnp.einsum('gts,bnsgc->bntgc', w, vc) + b_s.T[None, None, :, :, None]
    return u * mixed.reshape(bsz, seq, SG_WIDTH)


def moba_attention(q, k, v, rel_table):
    bsz, n_heads, seq, hd = q.shape
    n_blocks = -(-seq // MOBA_BLOCK)
    seq_pad = n_blocks * MOBA_BLOCK
    pad = ((0, 0), (0, 0), (0, seq_pad - seq), (0, 0))
    q, k, v = jnp.pad(q, pad), jnp.pad(k, pad), jnp.pad(v, pad)
    topk = min(MOBA_TOPK, n_blocks)
    scale = hd ** -0.5
    k_blk = k.reshape(bsz, n_heads, n_blocks, MOBA_BLOCK, hd)
    v_blk = v.reshape(bsz, n_heads, n_blocks, MOBA_BLOCK, hd)

    pos = jnp.arange(seq_pad, dtype=jnp.int32)
    q_block = pos // MOBA_BLOCK
    k_mean = jnp.mean(k_blk.astype(jnp.float32), axis=3)
    gate = jnp.einsum('bhsd,bhnd->bhsn', q.astype(jnp.float32), k_mean)
    fully_past = jnp.arange(n_blocks, dtype=jnp.int32)[None, :] < q_block[:, None]
    gate = jnp.where(fully_past, gate, NEG_INF)
    _, sel = lax.top_k(gate, topk)

    n_q = seq_pad // MOBA_QCHUNK
    q_c = q.reshape(bsz, n_heads, n_q, MOBA_QCHUNK, hd).transpose(2, 0, 1, 3, 4)
    sel_c = sel.reshape(bsz, n_heads, n_q, MOBA_QCHUNK, topk).transpose(2, 0, 1, 3, 4)
    table_t = rel_table.T.astype(jnp.float32)
    b_idx = jnp.arange(bsz)[:, None, None, None]
    h_idx = jnp.arange(n_heads)[None, :, None, None]
    h_idx5 = h_idx[..., None]
    offs = jnp.arange(MOBA_BLOCK, dtype=jnp.int32)

    def attend_chunk(args):
        ci, qc, selc = args
        t = ci * MOBA_QCHUNK + jnp.arange(MOBA_QCHUNK, dtype=jnp.int32)
        own = (ci * MOBA_QCHUNK) // MOBA_BLOCK
        k_own = lax.dynamic_index_in_dim(k_blk, own, axis=2, keepdims=False)
        v_own = lax.dynamic_index_in_dim(v_blk, own, axis=2, keepdims=False)
        d_own = t[:, None] - (own * MOBA_BLOCK + offs)[None, :]
        l_own = (jnp.einsum('bhqd,bhjd->bhqj', qc, k_own).astype(jnp.float32) * scale
                 + table_t[:, rel_bucket(d_own)][None])
        l_own = jnp.where((d_own >= 0)[None, None], l_own, NEG_INF)
        k_sel = k_blk[b_idx, h_idx, selc]
        v_sel = v_blk[b_idx, h_idx, selc]
        d_sel = t[None, None, :, None, None] - (selc[..., None] * MOBA_BLOCK + offs)
        l_sel = (jnp.einsum('bhqd,bhqkjd->bhqkj', qc, k_sel).astype(jnp.float32) * scale
                 + table_t[h_idx5, rel_bucket(d_sel)])
        slot_ok = jnp.arange(topk, dtype=jnp.int32)[None, :] < (t // MOBA_BLOCK)[:, None]
        l_sel = jnp.where(slot_ok[None, None, :, :, None], l_sel, NEG_INF)
        logits = jnp.concatenate(
            [l_sel.reshape(bsz, n_heads, MOBA_QCHUNK, topk * MOBA_BLOCK), l_own], axis=-1)
        p = jax.nn.softmax(logits, axis=-1).astype(v.dtype)
        p_sel = p[..., :topk * MOBA_BLOCK].reshape(bsz, n_heads, MOBA_QCHUNK, topk, MOBA_BLOCK)
        p_own = p[..., topk * MOBA_BLOCK:]
        return (jnp.einsum('bhqkj,bhqkjd->bhqd', p_sel, v_sel)
                + jnp.einsum('bhqj,bhjd->bhqd', p_own, v_own))

    out = lax.map(attend_chunk, (jnp.arange(n_q, dtype=jnp.int32), q_c, sel_c))
    out = out.transpose(1, 2, 0, 3, 4).reshape(bsz, n_heads, seq_pad, hd)
    return out[:, :, :seq]


def token_mixer(h, w_in, sg_ln_g, sg_ln_b, sg_w_s, sg_b_s, rel_table, w_proj_a, w_proj_b, w_out):
    bsz, seq, _ = h.shape
    z = h @ w_in
    o1 = 2 * SG_WIDTH
    o2 = o1 + 3 * MOBA_WIDTH
    uv = jax.nn.gelu(z[..., :o1])
    y_a = spatial_gating(uv[..., :SG_WIDTH], uv[..., SG_WIDTH:], sg_ln_g, sg_ln_b, sg_w_s, sg_b_s)
    qkv = z[..., o1:o2].reshape(bsz, seq, 3, MOBA_HEADS, MOBA_HEAD_DIM).transpose(2, 0, 3, 1, 4)
    y_b = moba_attention(qkv[0], qkv[1], qkv[2], rel_table)
    y_b = y_b.transpose(0, 2, 1, 3).reshape(bsz, seq, MOBA_WIDTH)
    gates = jax.nn.sigmoid(z[..., o2:])
    merged = gates[..., :D_MODEL] * (y_a @ w_proj_a) + gates[..., D_MODEL:] * (y_b @ w_proj_b)
    return merged @ w_out


def setup_inputs(seed: int = 0) -> dict:
    key = jax.random.key(seed)
    ks = jax.random.split(key, 26)
    L = DEPTH

    def nrm(k, shape, scale):
        return jax.random.normal(k, shape, jnp.float32) * scale

    d_in = D_MODEL ** -0.5
    beta = DEEPNORM_BETA
    x = nrm(ks[0], (BATCH, SEQ, D_MODEL), 1.0)
    ffn1_w_gate = nrm(ks[1], (L, D_MODEL, D_FF), d_in)
    ffn1_w_up = nrm(ks[2], (L, D_MODEL, D_FF), d_in * beta)
    ffn1_w_down = nrm(ks[3], (L, D_FF, D_MODEL), D_FF ** -0.5 * beta)
    ln1_g = 1.0 + nrm(ks[4], (L, D_MODEL), 0.02)
    ln1_b = nrm(ks[5], (L, D_MODEL), 0.02)
    w_in = jnp.concatenate([
        nrm(ks[6], (L, D_MODEL, 2 * SG_WIDTH), d_in),
        nrm(ks[7], (L, D_MODEL, 2 * MOBA_WIDTH), d_in),
        nrm(ks[8], (L, D_MODEL, MOBA_WIDTH), d_in * beta),
        nrm(ks[9], (L, D_MODEL, 2 * D_MODEL), d_in),
    ], axis=-1)
    sg_ln_g = 1.0 + nrm(ks[10], (L, SG_WIDTH), 0.02)
    sg_ln_b = nrm(ks[11], (L, SG_WIDTH), 0.02)
    sg_w_s = nrm(ks[12], (L, SG_GROUPS, SG_CHUNK, SG_CHUNK), SG_CHUNK ** -0.5)
    sg_b_s = 1.0 + nrm(ks[13], (L, SG_GROUPS, SG_CHUNK), 0.02)
    rel_table = nrm(ks[14], (REL_BUCKETS, MOBA_HEADS), 0.1)
    w_proj_a = nrm(ks[15], (L, SG_WIDTH, D_MODEL), SG_WIDTH ** -0.5 * beta)
    w_proj_b = nrm(ks[16], (L, MOBA_WIDTH, D_MODEL), MOBA_WIDTH ** -0.5 * beta)
    w_out = nrm(ks[17], (L, D_MODEL, D_MODEL), d_in * beta)
    ln2_g = 1.0 + nrm(ks[18], (L, D_MODEL), 0.02)
    ln2_b = nrm(ks[19], (L, D_MODEL), 0.02)
    ffn2_w_gate = nrm(ks[20], (L, D_MODEL, D_FF), d_in)
    ffn2_w_up = nrm(ks[21], (L, D_MODEL, D_FF), d_in * beta)
    ffn2_w_down = nrm(ks[22], (L, D_FF, D_MODEL), D_FF ** -0.5 * beta)
    ln3_g = 1.0 + nrm(ks[23], (L, D_MODEL), 0.02)
    ln3_b = nrm(ks[24], (L, D_MODEL), 0.02)
    return {'x': x, 'ffn1_w_gate': ffn1_w_gate, 'ffn1_w_up': ffn1_w_up, 'ffn1_w_down': ffn1_w_down,
            'ln1_g': ln1_g, 'ln1_b': ln1_b, 'w_in': w_in, 'sg_ln_g': sg_ln_g, 'sg_ln_b': sg_ln_b,
            'sg_w_s': sg_w_s, 'sg_b_s': sg_b_s, 'rel_table': rel_table, 'w_proj_a': w_proj_a,
            'w_proj_b': w_proj_b, 'w_out': w_out, 'ln2_g': ln2_g, 'ln2_b': ln2_b,
            'ffn2_w_gate': ffn2_w_gate, 'ffn2_w_up': ffn2_w_up, 'ffn2_w_down': ffn2_w_down,
            'ln3_g': ln3_g, 'ln3_b': ln3_b}


def reference(x, ffn1_w_gate, ffn1_w_up, ffn1_w_down, ln1_g, ln1_b, w_in, sg_ln_g, sg_ln_b,
              sg_w_s, sg_b_s, rel_table, w_proj_a, w_proj_b, w_out, ln2_g, ln2_b,
              ffn2_w_gate, ffn2_w_up, ffn2_w_down, ln3_g, ln3_b):
    h = x
    for l in range(DEPTH):
        f1 = swiglu_ffn(h, ffn1_w_gate[l], ffn1_w_up[l], ffn1_w_down[l])
        h = layer_norm(DEEPNORM_ALPHA * h + MACARON_WEIGHT * f1, ln1_g[l], ln1_b[l])
        m = token_mixer(h, w_in[l], sg_ln_g[l], sg_ln_b[l], sg_w_s[l], sg_b_s[l], rel_table,
                        w_proj_a[l], w_proj_b[l], w_out[l])
        h = layer_norm(DEEPNORM_ALPHA * h + m, ln2_g[l], ln2_b[l])
        f2 = swiglu_ffn(h, ffn2_w_gate[l], ffn2_w_up[l], ffn2_w_down[l])
        h = layer_norm(DEEPNORM_ALPHA * h + MACARON_WEIGHT * f2, ln3_g[l], ln3_b[l])
    return h
```

```python
import functools
import math

import jax
import jax.numpy as jnp
import numpy as np
from jax import lax
from jax.experimental import pallas as pl
from jax.experimental.pallas import tpu as pltpu

D_MODEL = 1024
D_FF = 2816
DEPTH = 1
SG_GROUPS = 8
SG_CHUNK = 128
SG_WIDTH = 1024
MOBA_HEADS = 8
MOBA_HEAD_DIM = 128
MOBA_WIDTH = 1024
MOBA_BLOCK = 256
MOBA_TOPK = 3
REL_BUCKETS = 32
REL_MAX_DIST = 128
MACARON_WEIGHT = 0.5
DEEPNORM_ALPHA = (2.0 * DEPTH) ** 0.25
LN_EPS = 1e-5
NEG_INF = -1e30
LOG2E = math.log2(math.e)
Q_SCALE = MOBA_HEAD_DIM ** -0.5 * LOG2E

VMEM_LIMIT_BYTES = 56 * 1024 * 1024

F32 = jnp.float32
BF16 = jnp.bfloat16

TOKEN_TILE = 256


def _layer_norm(y, g, b):
    mu = jnp.mean(y, axis=-1, keepdims=True)
    d = y - mu
    var = jnp.mean(d * d, axis=-1, keepdims=True)
    return d * lax.rsqrt(var + LN_EPS) * g + b


def _dot(a, b):
    return jnp.dot(a, b, preferred_element_type=F32)


def _swiglu(xb, wg_ref, wu_ref, wd_ref):
    gate = _dot(xb, wg_ref[...])
    up = _dot(xb, wu_ref[...])
    act = (gate * jax.nn.sigmoid(gate) * up).astype(BF16)
    return _dot(act, wd_ref[...])


def _resident(shape):
    zeros = (0,) * len(shape)
    return pl.BlockSpec(shape, lambda *_: zeros, pipeline_mode=pl.Buffered(1))


def _ffn_ln_kernel(x_ref, wg_ref, wu_ref, wd_ref, g_ref, b_ref, o_ref):
    x = x_ref[...]
    f = _swiglu(x.astype(BF16), wg_ref, wu_ref, wd_ref)
    o_ref[...] = _layer_norm(DEEPNORM_ALPHA * x + MACARON_WEIGHT * f, g_ref[...], b_ref[...])


def _ffn_ln(x2d, wg, wu, wd, g, b):
    n_tok = x2d.shape[0]
    tm = TOKEN_TILE
    return pl.pallas_call(
        _ffn_ln_kernel,
        grid=(n_tok // tm,),
        in_specs=[
            pl.BlockSpec((tm, D_MODEL), lambda t: (t, 0)),
            _resident((D_MODEL, D_FF)),
            _resident((D_MODEL, D_FF)),
            _resident((D_FF, D_MODEL)),
            _resident((1, D_MODEL)),
            _resident((1, D_MODEL)),
        ],
        out_specs=pl.BlockSpec((tm, D_MODEL), lambda t: (t, 0)),
        out_shape=jax.ShapeDtypeStruct((n_tok, D_MODEL), F32),
        compiler_params=pltpu.CompilerParams(
            dimension_semantics=("parallel",), vmem_limit_bytes=VMEM_LIMIT_BYTES),
        name="ffn_ln",
    )(x2d, wg, wu, wd, g, b)


def _mixer_in_kernel(h_ref, wu_ref, wv_ref, wq_ref, wk_ref, wvm_ref, wga_ref, lng_ref, lnb_ref,
                     ws_ref, bs_ref, wpa_ref, pa_ref, q_ref, k_ref, vt_ref, ya_ref):
    tm = h_ref.shape[1]
    n_chunks = tm // SG_CHUNK
    hb = h_ref[0].astype(BF16)

    u = jax.nn.gelu(_dot(hb, wu_ref[...]))
    v = _layer_norm(jax.nn.gelu(_dot(hb, wv_ref[...])), lng_ref[...], lnb_ref[...]).astype(BF16)
    row = lax.broadcasted_iota(jnp.int32, (SG_CHUNK, SG_CHUNK), 0)
    col = lax.broadcasted_iota(jnp.int32, (SG_CHUNK, SG_CHUNK), 1)
    causal = col <= row
    for g in range(SG_GROUPS):
        cs = slice(g * SG_CHUNK, (g + 1) * SG_CHUNK)
        w = jnp.where(causal, ws_ref[g], 0.0).astype(BF16)
        rhs = jnp.concatenate([v[t * SG_CHUNK:(t + 1) * SG_CHUNK, cs] for t in range(n_chunks)], axis=1)
        mixed = _dot(w, rhs)
        for t in range(n_chunks):
            ts = slice(t * SG_CHUNK, (t + 1) * SG_CHUNK)
            ya_ref[ts, cs] = (u[ts, cs] * (mixed[:, ts] + bs_ref[g])).astype(BF16)

    gate_a = jax.nn.sigmoid(_dot(hb, wga_ref[...]))
    pa_ref[0] = (gate_a * _dot(ya_ref[...], wpa_ref[...])).astype(BF16)

    zq = _dot(hb, wq_ref[...]) * Q_SCALE
    zk = _dot(hb, wk_ref[...])
    zv = _dot(hb, wvm_ref[...])
    for h in range(MOBA_HEADS):
        hs = slice(h * MOBA_HEAD_DIM, (h + 1) * MOBA_HEAD_DIM)
        q_ref[0, h] = zq[:, hs].astype(BF16)
        k_ref[0, h] = zk[:, hs].astype(BF16)
        vt_ref[0, h] = zv[:, hs].T.astype(BF16)


def _mixer_in(h1, w_u, w_v, w_q, w_k, w_vm, w_ga, ln_g, ln_b, w_s, b_s_bcast, w_pa):
    bsz, seq, _ = h1.shape
    tm = TOKEN_TILE
    sq = jax.ShapeDtypeStruct
    wspec = _resident((D_MODEL, D_MODEL))
    return pl.pallas_call(
        _mixer_in_kernel,
        grid=(bsz, seq // tm),
        in_specs=[
            pl.BlockSpec((1, tm, D_MODEL), lambda b, t: (b, t, 0)),
            wspec, wspec, wspec, wspec, wspec, wspec,
            _resident((1, SG_WIDTH)),
            _resident((1, SG_WIDTH)),
            _resident((SG_GROUPS, SG_CHUNK, SG_CHUNK)),
            _resident((SG_GROUPS, SG_CHUNK, SG_CHUNK)),
            wspec,
        ],
        out_specs=[
            pl.BlockSpec((1, tm, D_MODEL), lambda b, t: (b, t, 0)),
            pl.BlockSpec((1, MOBA_HEADS, tm, MOBA_HEAD_DIM), lambda b, t: (b, 0, t, 0)),
            pl.BlockSpec((1, MOBA_HEADS, tm, MOBA_HEAD_DIM), lambda b, t: (b, 0, t, 0)),
            pl.BlockSpec((1, MOBA_HEADS, MOBA_HEAD_DIM, tm), lambda b, t: (b, 0, 0, t)),
        ],
        out_shape=[
            sq((bsz, seq, D_MODEL), BF16),
            sq((bsz, MOBA_HEADS, seq, MOBA_HEAD_DIM), BF16),
            sq((bsz, MOBA_HEADS, seq, MOBA_HEAD_DIM), BF16),
            sq((bsz, MOBA_HEADS, MOBA_HEAD_DIM, seq), BF16),
        ],
        scratch_shapes=[pltpu.VMEM((tm, SG_WIDTH), BF16)],
        compiler_params=pltpu.CompilerParams(
            dimension_semantics=("parallel", "parallel"), vmem_limit_bytes=VMEM_LIMIT_BYTES),
        name="mixer_in",
    )(h1, w_u, w_v, w_q, w_k, w_vm, w_ga, ln_g, ln_b, w_s, b_s_bcast, w_pa)


_NT = (((1,), (1,)), ((), ()))


def _moba_kernel(q_ref, k_ref, vt_ref, bias_ref, cfar_ref, o_ref, kmh_ref, kml_ref, mfar_ref, mnear_ref):
    i = pl.program_id(2)
    n_blocks = k_ref.shape[2] // MOBA_BLOCK
    blk = MOBA_BLOCK

    @pl.when(i == 0)
    def _():
        k32 = k_ref[0, 0].astype(F32)
        km = jnp.sum(k32.reshape(n_blocks, blk, MOBA_HEAD_DIM), axis=1) * (1.0 / blk)
        hi = km.astype(BF16)
        kmh_ref[...] = hi
        kml_ref[...] = (km - hi.astype(F32)).astype(BF16)

    q = q_ref[0, 0]

    gate = (lax.dot_general(kmh_ref[...], q, _NT, preferred_element_type=F32)
            + lax.dot_general(kml_ref[...], q, _NT, preferred_element_type=F32))
    blk_id = lax.broadcasted_iota(jnp.int32, (n_blocks, blk), 0)
    past = blk_id < i
    gate = jnp.where(past, gate, NEG_INF)
    cfar = cfar_ref[0]
    far_rows, near_rows = [], []
    for n in range(n_blocks):
        gn = gate[n:n + 1, :]
        beats = jnp.where(gate > gn, 1.0, jnp.where((gate == gn) & (blk_id < n), 1.0, 0.0))
        rank = jnp.sum(jnp.where(past, beats, 0.0), axis=0, keepdims=True)
        sel = (rank < float(MOBA_TOPK)) & (i > n)
        far_rows.append(jnp.where(sel, cfar, NEG_INF))
        near_rows.append(jnp.where(sel, 0.0, NEG_INF))
    mfar_ref[...] = jnp.concatenate(far_rows, axis=0)
    mnear_ref[...] = jnp.concatenate(near_rows, axis=0)

    def scores(n):
        start = pl.multiple_of(n * blk, blk)
        kb = k_ref[0, 0, pl.ds(start, blk), :]
        return lax.dot_general(kb, q, _NT, preferred_element_type=F32)

    def pv(n, p):
        start = pl.multiple_of(n * blk, blk)
        return _dot(vt_ref[0, 0, :, pl.ds(start, blk)], p.astype(BF16))

    def update(carry, n, s):
        m, l, acc = carry
        m_new = jnp.maximum(m, jnp.max(s, axis=0, keepdims=True))
        alpha = jnp.exp2(m - m_new)
        p = jnp.exp2(s - m_new)
        return m_new, alpha * l + jnp.sum(p, axis=0, keepdims=True), alpha * acc + pv(n, p)

    s = scores(i) + bias_ref[0, 0]
    m = jnp.max(s, axis=0, keepdims=True)
    p = jnp.exp2(s - m)
    carry = (m, jnp.sum(p, axis=0, keepdims=True), pv(i, p))

    def prev_block(c):
        n = i - 1
        return update(c, n, scores(n) + bias_ref[0, 1] + mnear_ref[pl.ds(n, 1), :])

    carry = lax.cond(i >= 1, prev_block, lambda c: c, carry)

    def far_block(n, c):
        return update(c, n, scores(n) + mfar_ref[pl.ds(n, 1), :])

    _, l, acc = lax.fori_loop(0, jnp.maximum(i - 1, 0), far_block, carry)
    o_ref[0] = (acc * (1.0 / l)).T.astype(BF16)


def _moba(q, k, vt, bias_tiles, cfar):
    bsz, n_heads, seq, hd = q.shape
    n_blocks = seq // MOBA_BLOCK
    return pl.pallas_call(
        _moba_kernel,
        grid=(bsz, n_heads, n_blocks),
        in_specs=[
            pl.BlockSpec((1, 1, MOBA_BLOCK, hd), lambda b, h, i: (b, h, i, 0)),
            pl.BlockSpec((1, 1, seq, hd), lambda b, h, i: (b, h, 0, 0)),
            pl.BlockSpec((1, 1, hd, seq), lambda b, h, i: (b, h, 0, 0)),
            pl.BlockSpec((1, 2, MOBA_BLOCK, MOBA_BLOCK), lambda b, h, i: (h, 0, 0, 0)),
            pl.BlockSpec((1, 1, MOBA_BLOCK), lambda b, h, i: (h, 0, 0)),
        ],
        out_specs=pl.BlockSpec((1, MOBA_BLOCK, hd), lambda b, h, i: (b, i, h)),
        out_shape=jax.ShapeDtypeStruct((bsz, seq, n_heads * hd), BF16),
        scratch_shapes=[
            pltpu.VMEM((n_blocks, hd), BF16),
            pltpu.VMEM((n_blocks, hd), BF16),
            pltpu.VMEM((n_blocks, MOBA_BLOCK), F32),
            pltpu.VMEM((n_blocks, MOBA_BLOCK), F32),
        ],
        compiler_params=pltpu.CompilerParams(
            dimension_semantics=("parallel", "parallel", "arbitrary"),
            vmem_limit_bytes=VMEM_LIMIT_BYTES),
        name="moba",
    )(q, k, vt, bias_tiles, cfar)


def _rel_bucket_table():
    n = np.arange(2 * MOBA_BLOCK, dtype=np.int32)
    max_exact = REL_BUCKETS // 2
    nf = np.maximum(n, 1).astype(np.float32)
    ratio = np.log(nf / np.float32(max_exact)) / np.float32(math.log(REL_MAX_DIST / max_exact))
    large = max_exact + (ratio * np.float32(REL_BUCKETS - max_exact)).astype(np.int32)
    large = np.minimum(large, REL_BUCKETS - 1)
    return np.where(n < max_exact, n, large).astype(np.int32)


def _rel_bias_tiles(rel_table):
    bucket = _rel_bucket_table()
    key = np.arange(MOBA_BLOCK)[:, None]
    qry = np.arange(MOBA_BLOCK)[None, :]
    d_own = qry - key
    table_t = rel_table.T.astype(F32) * LOG2E
    own = jnp.where(jnp.asarray(d_own >= 0)[None], table_t[:, bucket[np.maximum(d_own, 0)]], NEG_INF)
    prev = table_t[:, bucket[d_own + MOBA_BLOCK]]
    assert int(bucket[MOBA_BLOCK + 1]) == REL_BUCKETS - 1
    cfar = jnp.broadcast_to(table_t[:, REL_BUCKETS - 1][:, None, None], (MOBA_HEADS, 1, MOBA_BLOCK))
    return jnp.stack([own, prev], axis=1), cfar


def _merge_ffn_kernel(h_ref, pa_ref, yb_ref, wgb_ref, wpb_ref, wout_ref, g2_ref, b2_ref,
                      wg_ref, wu_ref, wd_ref, g3_ref, b3_ref, o_ref):
    h = h_ref[...]
    gate_b = jax.nn.sigmoid(_dot(h.astype(BF16), wgb_ref[...]))
    merged = pa_ref[...].astype(F32) + gate_b * _dot(yb_ref[...], wpb_ref[...])
    mixed = _dot(merged.astype(BF16), wout_ref[...])
    h2 = _layer_norm(DEEPNORM_ALPHA * h + mixed, g2_ref[...], b2_ref[...])
    f = _swiglu(h2.astype(BF16), wg_ref, wu_ref, wd_ref)
    o_ref[...] = _layer_norm(DEEPNORM_ALPHA * h2 + MACARON_WEIGHT * f, g3_ref[...], b3_ref[...])


def _merge_ffn(h1, pa, yb, w_gb, w_pb, w_out, g2, b2, wg, wu, wd, g3, b3):
    n_tok = h1.shape[0]
    tm = TOKEN_TILE
    tok = lambda: pl.BlockSpec((tm, D_MODEL), lambda t: (t, 0))
    wspec = _resident((D_MODEL, D_MODEL))
    vec = _resident((1, D_MODEL))
    return pl.pallas_call(
        _merge_ffn_kernel,
        grid=(n_tok // tm,),
        in_specs=[tok(), tok(), tok(), wspec, wspec, wspec, vec, vec,
                  _resident((D_MODEL, D_FF)), _resident((D_MODEL, D_FF)), _resident((D_FF, D_MODEL)),
                  vec, vec],
        out_specs=tok(),
        out_shape=jax.ShapeDtypeStruct((n_tok, D_MODEL), F32),
        compiler_params=pltpu.CompilerParams(
            dimension_semantics=("parallel",), vmem_limit_bytes=VMEM_LIMIT_BYTES),
        name="merge_ffn",
    )(h1, pa, yb, w_gb, w_pb, w_out, g2, b2, wg, wu, wd, g3, b3)


def kernel(x, ffn1_w_gate, ffn1_w_up, ffn1_w_down, ln1_g, ln1_b, w_in, sg_ln_g, sg_ln_b, sg_w_s, sg_b_s,
           rel_table, w_proj_a, w_proj_b, w_out, ln2_g, ln2_b, ffn2_w_gate, ffn2_w_up, ffn2_w_down,
           ln3_g, ln3_b):
    bsz, seq, d = x.shape
    n_tok = bsz * seq
    bias_tiles, cfar = _rel_bias_tiles(rel_table)
    vec = lambda a: a.reshape(1, -1).astype(F32)
    h = x.reshape(n_tok, d)
    for l in range(DEPTH):
        h = _ffn_ln(h, ffn1_w_gate[l].astype(BF16), ffn1_w_up[l].astype(BF16),
                    ffn1_w_down[l].astype(BF16), vec(ln1_g[l]), vec(ln1_b[l]))
        wi = w_in[l].astype(BF16)
        cols = [wi[:, c * D_MODEL:(c + 1) * D_MODEL] for c in range(7)]
        b_s_bcast = jnp.broadcast_to(sg_b_s[l][:, :, None], (SG_GROUPS, SG_CHUNK, SG_CHUNK)).astype(F32)
        pa, q, k, vt = _mixer_in(h.reshape(bsz, seq, d), cols[0], cols[1], cols[2], cols[3], cols[4],
                                 cols[5], vec(sg_ln_g[l]), vec(sg_ln_b[l]), sg_w_s[l].astype(F32),
                                 b_s_bcast, w_proj_a[l].astype(BF16))
        yb = _moba(q, k, vt, bias_tiles, cfar)
        h = _merge_ffn(h, pa.reshape(n_tok, d), yb.reshape(n_tok, d), cols[6], w_proj_b[l].astype(BF16),
                       w_out[l].astype(BF16), vec(ln2_g[l]), vec(ln2_b[l]),
                       ffn2_w_gate[l].astype(BF16), ffn2_w_up[l].astype(BF16),
                       ffn2_w_down[l].astype(BF16), vec(ln3_g[l]), vec(ln3_b[l]))
    return h.reshape(bsz, seq, d)
```

```python
import functools
import math

import jax
import jax.numpy as jnp
import numpy as np
from jax import lax
from jax.experimental import pallas as pl
from jax.experimental.pallas import tpu as pltpu

D_MODEL = 1024
D_FF = 2816
DEPTH = 1
SG_GROUPS = 8
SG_CHUNK = 128
SG_WIDTH = 1024
MOBA_HEADS = 8
MOBA_HEAD_DIM = 128
MOBA_WIDTH = 1024
MOBA_BLOCK = 256
MOBA_TOPK = 3
REL_BUCKETS = 32
REL_MAX_DIST = 128
MACARON_WEIGHT = 0.5
DEEPNORM_ALPHA = (2.0 * DEPTH) ** 0.25
LN_EPS = 1e-5
NEG_INF = -1e30
LOG2E = math.log2(math.e)
Q_SCALE = MOBA_HEAD_DIM ** -0.5 * LOG2E

VMEM_LIMIT_BYTES = 56 * 1024 * 1024

F32 = jnp.float32
BF16 = jnp.bfloat16

TOKEN_TILE = 256


def _layer_norm(y, g, b):
    mu = jnp.mean(y, axis=-1, keepdims=True)
    d = y - mu
    var = jnp.mean(d * d, axis=-1, keepdims=True)
    return d * lax.rsqrt(var + LN_EPS) * g + b


def _dot(a, b):
    return jnp.dot(a, b, preferred_element_type=F32)


def _swiglu(xb, wg_ref, wu_ref, wd_ref):
    gate = _dot(xb, wg_ref[...])
    up = _dot(xb, wu_ref[...])
    act = (gate * jax.nn.sigmoid(gate) * up).astype(BF16)
    return _dot(act, wd_ref[...])


def _resident(shape):
    zeros = (0,) * len(shape)
    return pl.BlockSpec(shape, lambda *_: zeros, pipeline_mode=pl.Buffered(1))


def _ffn_ln_kernel(x_ref, wg_ref, wu_ref, wd_ref, g_ref, b_ref, o_ref):
    x = x_ref[...]
    f = _swiglu(x.astype(BF16), wg_ref, wu_ref, wd_ref)
    o_ref[...] = _layer_norm(DEEPNORM_ALPHA * x + MACARON_WEIGHT * f, g_ref[...], b_ref[...])


def _ffn_ln(x2d, wg, wu, wd, g, b):
    n_tok = x2d.shape[0]
    tm = TOKEN_TILE
    return pl.pallas_call(
        _ffn_ln_kernel,
        grid=(n_tok // tm,),
        in_specs=[
            pl.BlockSpec((tm, D_MODEL), lambda t: (t, 0)),
            _resident((D_MODEL, D_FF)),
            _resident((D_MODEL, D_FF)),
            _resident((D_FF, D_MODEL)),
            _resident((1, D_MODEL)),
            _resident((1, D_MODEL)),
        ],
        out_specs=pl.BlockSpec((tm, D_MODEL), lambda t: (t, 0)),
        out_shape=jax.ShapeDtypeStruct((n_tok, D_MODEL), F32),
        compiler_params=pltpu.CompilerParams(
            dimension_semantics=("parallel",), vmem_limit_bytes=VMEM_LIMIT_BYTES),
        name="ffn_ln",
    )(x2d, wg, wu, wd, g, b)


def _mixer_in_kernel(h_ref, wu_ref, wv_ref, wq_ref, wk_ref, wvm_ref, wga_ref, lng_ref, lnb_ref,
                     ws_ref, bs_ref, wpa_ref, pa_ref, q_ref, k_ref, vt_ref, ya_ref):
    tm = h_ref.shape[1]
    n_chunks = tm // SG_CHUNK
    hb = h_ref[0].astype(BF16)

    u = jax.nn.gelu(_dot(hb, wu_ref[...]))
    v = _layer_norm(jax.nn.gelu(_dot(hb, wv_ref[...])), lng_ref[...], lnb_ref[...]).astype(BF16)
    row = lax.broadcasted_iota(jnp.int32, (SG_CHUNK, SG_CHUNK), 0)
    col = lax.broadcasted_iota(jnp.int32, (SG_CHUNK, SG_CHUNK), 1)
    causal = col <= row
    for g in range(SG_GROUPS):
        cs = slice(g * SG_CHUNK, (g + 1) * SG_CHUNK)
        w = jnp.where(causal, ws_ref[g], 0.0).astype(BF16)
        rhs = jnp.concatenate([v[t * SG_CHUNK:(t + 1) * SG_CHUNK, cs] for t in range(n_chunks)], axis=1)
        mixed = _dot(w, rhs)
        for t in range(n_chunks):
            ts = slice(t * SG_CHUNK, (t + 1) * SG_CHUNK)
            ya_ref[ts, cs] = (u[ts, cs] * (mixed[:, ts] + bs_ref[g])).astype(BF16)

    gate_a = jax.nn.sigmoid(_dot(hb, wga_ref[...]))
    pa_ref[0] = (gate_a * _dot(ya_ref[...], wpa_ref[...])).astype(BF16)

    zq = _dot(hb, wq_ref[...]) * Q_SCALE
    zk = _dot(hb, wk_ref[...])
    zv = _dot(hb, wvm_ref[...])
    for h in range(MOBA_HEADS):
        hs = slice(h * MOBA_HEAD_DIM, (h + 1) * MOBA_HEAD_DIM)
        q_ref[0, h] = zq[:, hs].astype(BF16)
        k_ref[0, h] = zk[:, hs].astype(BF16)
        vt_ref[0, h] = zv[:, hs].T.astype(BF16)


def _mixer_in(h1, w_u, w_v, w_q, w_k, w_vm, w_ga, ln_g, ln_b, w_s, b_s_bcast, w_pa):
    bsz, seq, _ = h1.shape
    tm = TOKEN_TILE
    sq = jax.ShapeDtypeStruct
    wspec = _resident((D_MODEL, D_MODEL))
    return pl.pallas_call(
        _mixer_in_kernel,
        grid=(bsz, seq // tm),
        in_specs=[
            pl.BlockSpec((1, tm, D_MODEL), lambda b, t: (b, t, 0)),
            wspec, wspec, wspec, wspec, wspec, wspec,
            _resident((1, SG_WIDTH)),
            _resident((1, SG_WIDTH)),
            _resident((SG_GROUPS, SG_CHUNK, SG_CHUNK)),
            _resident((SG_GROUPS, SG_CHUNK, SG_CHUNK)),
            wspec,
        ],
        out_specs=[
            pl.BlockSpec((1, tm, D_MODEL), lambda b, t: (b, t, 0)),
            pl.BlockSpec((1, MOBA_HEADS, tm, MOBA_HEAD_DIM), lambda b, t: (b, 0, t, 0)),
            pl.BlockSpec((1, MOBA_HEADS, tm, MOBA_HEAD_DIM), lambda b, t: (b, 0, t, 0)),
            pl.BlockSpec((1, MOBA_HEADS, MOBA_HEAD_DIM, tm), lambda b, t: (b, 0, 0, t)),
        ],
        out_shape=[
            sq((bsz, seq, D_MODEL), BF16),
            sq((bsz, MOBA_HEADS, seq, MOBA_HEAD_DIM), BF16),
            sq((bsz, MOBA_HEADS, seq, MOBA_HEAD_DIM), BF16),
            sq((bsz, MOBA_HEADS, MOBA_HEAD_DIM, seq), BF16),
        ],
        scratch_shapes=[pltpu.VMEM((tm, SG_WIDTH), BF16)],
        compiler_params=pltpu.CompilerParams(
            dimension_semantics=("parallel", "parallel"), vmem_limit_bytes=VMEM_LIMIT_BYTES),
        name="mixer_in",
    )(h1, w_u, w_v, w_q, w_k, w_vm, w_ga, ln_g, ln_b, w_s, b_s_bcast, w_pa)


_NT = (((1,), (1,)), ((), ()))


def _nt_dot(a, b):
    return lax.dot_general(a, b, _NT, preferred_element_type=F32)


def _moba_kernel(q_ref, k_ref, vt_ref, bias_ref, cfar_ref, o_ref):
    seq, hd = k_ref.shape[2], k_ref.shape[3]
    blk = MOBA_BLOCK
    n_blocks = seq // blk

    km = jnp.sum(k_ref[0, 0].astype(F32).reshape(n_blocks, blk, hd), axis=1) * (1.0 / blk)
    km_hi = km.astype(BF16)
    km_lo = (km - km_hi.astype(F32)).astype(BF16)
    q_all = q_ref[0, 0]
    gate = _nt_dot(km_hi, q_all) + _nt_dot(km_lo, q_all)

    blk_id = lax.broadcasted_iota(jnp.int32, (n_blocks, seq), 0)
    q_blk = lax.shift_right_logical(lax.broadcasted_iota(jnp.int32, (n_blocks, seq), 1),
                                    int(math.log2(blk)))
    gate = jnp.where(blk_id < q_blk, gate, NEG_INF)
    qb_row = q_blk[0:1]
    cfar = cfar_ref[0]
    far_rows = []
    near_row = jnp.full((1, seq), NEG_INF, F32)
    for n in range(n_blocks - 1):
        gn = gate[n:n + 1]
        beats = jnp.where(gate > gn, 1.0, jnp.where((gate == gn) & (blk_id < n), 1.0, 0.0))
        rank = jnp.sum(beats, axis=0, keepdims=True)
        sel = (rank < float(MOBA_TOPK)) & (qb_row > n)
        far_rows.append(jnp.where(sel & (qb_row > n + 1), cfar, NEG_INF))
        near_row = jnp.where(sel & (qb_row == n + 1), 0.0, near_row)

    for i in range(n_blocks):
        qs = slice(i * blk, (i + 1) * blk)
        n_keys = (i + 1) * blk
        s = _nt_dot(k_ref[0, 0, 0:n_keys, :], q_ref[0, 0, qs, :])
        pieces = []
        for n in range(i + 1):
            sn = s[n * blk:(n + 1) * blk]
            if n == i:
                sn = sn + bias_ref[0, 0]
            elif n == i - 1:
                sn = sn + bias_ref[0, 1] + near_row[:, qs]
            else:
                sn = sn + far_rows[n][:, qs]
            pieces.append(sn)
        m = jnp.max(functools.reduce(jnp.maximum, pieces), axis=0, keepdims=True)
        probs = [jnp.exp2(sn - m) for sn in pieces]
        l = jnp.sum(functools.reduce(jnp.add, probs), axis=0, keepdims=True)
        p = jnp.concatenate([x.astype(BF16) for x in probs], axis=0)
        acc = _dot(vt_ref[0, 0, :, 0:n_keys], p)
        o_ref[0, qs, :] = (acc * (1.0 / l)).T.astype(BF16)


def _moba(q, k, vt, bias_tiles, cfar):
    bsz, n_heads, seq, hd = q.shape
    head_major = lambda b, h: (b, h, 0, 0)
    return pl.pallas_call(
        _moba_kernel,
        grid=(bsz, n_heads),
        in_specs=[
            pl.BlockSpec((1, 1, seq, hd), head_major),
            pl.BlockSpec((1, 1, seq, hd), head_major),
            pl.BlockSpec((1, 1, hd, seq), head_major),
            pl.BlockSpec((1, 2, MOBA_BLOCK, MOBA_BLOCK), lambda b, h: (h, 0, 0, 0)),
            pl.BlockSpec((1, 1, seq), lambda b, h: (h, 0, 0)),
        ],
        out_specs=pl.BlockSpec((1, seq, hd), lambda b, h: (b, 0, h)),
        out_shape=jax.ShapeDtypeStruct((bsz, seq, n_heads * hd), BF16),
        compiler_params=pltpu.CompilerParams(
            dimension_semantics=("parallel", "parallel"), vmem_limit_bytes=VMEM_LIMIT_BYTES),
        name="moba",
    )(q, k, vt, bias_tiles, cfar)


def _rel_bucket_table():
    n = np.arange(2 * MOBA_BLOCK, dtype=np.int32)
    max_exact = REL_BUCKETS // 2
    nf = np.maximum(n, 1).astype(np.float32)
    ratio = np.log(nf / np.float32(max_exact)) / np.float32(math.log(REL_MAX_DIST / max_exact))
    large = max_exact + (ratio * np.float32(REL_BUCKETS - max_exact)).astype(np.int32)
    large = np.minimum(large, REL_BUCKETS - 1)
    return np.where(n < max_exact, n, large).astype(np.int32)


def _rel_bias_tiles(rel_table, seq):
    bucket = _rel_bucket_table()
    assert np.all(np.diff(bucket) >= 0) and int(bucket[MOBA_BLOCK + 1]) == REL_BUCKETS - 1
    key = np.arange(MOBA_BLOCK)[:, None]
    qry = np.arange(MOBA_BLOCK)[None, :]
    dist = jnp.asarray(np.stack([qry - key, qry - key + MOBA_BLOCK]))[None]
    table_t = (rel_table.T.astype(F32) * LOG2E)[:, :, None, None, None]
    tiles = jnp.broadcast_to(table_t[:, 0], (MOBA_HEADS, 2, MOBA_BLOCK, MOBA_BLOCK))
    for d in range(1, 2 * MOBA_BLOCK):
        if bucket[d] != bucket[d - 1]:
            tiles = jnp.where(dist >= d, table_t[:, int(bucket[d])], tiles)
    tiles = jnp.where(dist >= 0, tiles, NEG_INF)
    cfar = jnp.broadcast_to(table_t[:, REL_BUCKETS - 1, 0], (MOBA_HEADS, 1, seq))
    return tiles, cfar


def _merge_ffn_kernel(h_ref, pa_ref, yb_ref, wgb_ref, wpb_ref, wout_ref, g2_ref, b2_ref,
                      wg_ref, wu_ref, wd_ref, g3_ref, b3_ref, o_ref):
    h = h_ref[...]
    gate_b = jax.nn.sigmoid(_dot(h.astype(BF16), wgb_ref[...]))
    merged = pa_ref[...].astype(F32) + gate_b * _dot(yb_ref[...], wpb_ref[...])
    mixed = _dot(merged.astype(BF16), wout_ref[...])
    h2 = _layer_norm(DEEPNORM_ALPHA * h + mixed, g2_ref[...], b2_ref[...])
    f = _swiglu(h2.astype(BF16), wg_ref, wu_ref, wd_ref)
    o_ref[...] = _layer_norm(DEEPNORM_ALPHA * h2 + MACARON_WEIGHT * f, g3_ref[...], b3_ref[...])


def _merge_ffn(h1, pa, yb, w_gb, w_pb, w_out, g2, b2, wg, wu, wd, g3, b3):
    n_tok = h1.shape[0]
    tm = TOKEN_TILE
    tok = lambda: pl.BlockSpec((tm, D_MODEL), lambda t: (t, 0))
    wspec = _resident((D_MODEL, D_MODEL))
    vec = _resident((1, D_MODEL))
    return pl.pallas_call(
        _merge_ffn_kernel,
        grid=(n_tok // tm,),
        in_specs=[tok(), tok(), tok(), wspec, wspec, wspec, vec, vec,
                  _resident((D_MODEL, D_FF)), _resident((D_MODEL, D_FF)), _resident((D_FF, D_MODEL)),
                  vec, vec],
        out_specs=tok(),
        out_shape=jax.ShapeDtypeStruct((n_tok, D_MODEL), F32),
        compiler_params=pltpu.CompilerParams(
            dimension_semantics=("parallel",), vmem_limit_bytes=VMEM_LIMIT_BYTES),
        name="merge_ffn",
    )(h1, pa, yb, w_gb, w_pb, w_out, g2, b2, wg, wu, wd, g3, b3)


def kernel(x, ffn1_w_gate, ffn1_w_up, ffn1_w_down, ln1_g, ln1_b, w_in, sg_ln_g, sg_ln_b, sg_w_s, sg_b_s,
           rel_table, w_proj_a, w_proj_b, w_out, ln2_g, ln2_b, ffn2_w_gate, ffn2_w_up, ffn2_w_down,
           ln3_g, ln3_b):
    bsz, seq, d = x.shape
    n_tok = bsz * seq
    bias_tiles, cfar = _rel_bias_tiles(rel_table, seq)
    vec = lambda a: a.reshape(1, -1).astype(F32)
    h = x.reshape(n_tok, d)
    for l in range(DEPTH):
        h = _ffn_ln(h, ffn1_w_gate[l].astype(BF16), ffn1_w_up[l].astype(BF16),
                    ffn1_w_down[l].astype(BF16), vec(ln1_g[l]), vec(ln1_b[l]))
        wi = w_in[l].astype(BF16)
        cols = [wi[:, c * D_MODEL:(c + 1) * D_MODEL] for c in range(7)]
        b_s_bcast = jnp.broadcast_to(sg_b_s[l][:, :, None], (SG_GROUPS, SG_CHUNK, SG_CHUNK)).astype(F32)
        pa, q, k, vt = _mixer_in(h.reshape(bsz, seq, d), cols[0], cols[1], cols[2], cols[3], cols[4],
                                 cols[5], vec(sg_ln_g[l]), vec(sg_ln_b[l]), sg_w_s[l].astype(F32),
                                 b_s_bcast, w_proj_a[l].astype(BF16))
        yb = _moba(q, k, vt, bias_tiles, cfar)
        h = _merge_ffn(h, pa.reshape(n_tok, d), yb.reshape(n_tok, d), cols[6], w_proj_b[l].astype(BF16),
                       w_out[l].astype(BF16), vec(ln2_g[l]), vec(ln2_b[l]),
                       ffn2_w_gate[l].astype(BF16), ffn2_w_up[l].astype(BF16),
                       ffn2_w_down[l].astype(BF16), vec(ln3_g[l]), vec(ln3_b[l]))
    return h.reshape(bsz, seq, d)
```

```python
import functools
import math

import jax
import jax.numpy as jnp
import numpy as np
from jax import lax
from jax.experimental import pallas as pl
from jax.experimental.pallas import tpu as pltpu

D_MODEL = 1024
D_FF = 2816
DEPTH = 1
SG_GROUPS = 8
SG_CHUNK = 128
SG_WIDTH = 1024
MOBA_HEADS = 8
MOBA_HEAD_DIM = 128
MOBA_WIDTH = 1024
MOBA_BLOCK = 256
MOBA_TOPK = 3
REL_BUCKETS = 32
REL_MAX_DIST = 128
MACARON_WEIGHT = 0.5
DEEPNORM_ALPHA = (2.0 * DEPTH) ** 0.25
LN_EPS = 1e-5
NEG_INF = -1e30
LOG2E = math.log2(math.e)
Q_SCALE = MOBA_HEAD_DIM ** -0.5 * LOG2E

VMEM_LIMIT_BYTES = 56 * 1024 * 1024

F32 = jnp.float32
BF16 = jnp.bfloat16

FFN_TOKEN_TILE = 512
MIXER_TOKEN_TILE = 512
MERGE_TOKEN_TILE = 256
VT_ROWS = MOBA_HEAD_DIM + 16


def _layer_norm(y, g, b):
    mu = jnp.mean(y, axis=-1, keepdims=True)
    d = y - mu
    var = jnp.mean(d * d, axis=-1, keepdims=True)
    return d * lax.rsqrt(var + LN_EPS) * g + b


def _dot(a, b):
    return jnp.dot(a, b, preferred_element_type=F32)


def _swiglu(xb, wg_ref, wu_ref, wd_ref):
    gate = _dot(xb, wg_ref[...])
    up = _dot(xb, wu_ref[...])
    act = (gate * jax.nn.sigmoid(gate) * up).astype(BF16)
    return _dot(act, wd_ref[...])


def _resident(shape):
    zeros = (0,) * len(shape)
    return pl.BlockSpec(shape, lambda *_: zeros, pipeline_mode=pl.Buffered(1))


def _ffn_ln_kernel(x_ref, wg_ref, wu_ref, wd_ref, g_ref, b_ref, o_ref):
    x = x_ref[...]
    f = _swiglu(x.astype(BF16), wg_ref, wu_ref, wd_ref)
    o_ref[...] = _layer_norm(DEEPNORM_ALPHA * x + MACARON_WEIGHT * f, g_ref[...], b_ref[...])


def _ffn_ln(x2d, wg, wu, wd, g, b):
    n_tok = x2d.shape[0]
    tm = FFN_TOKEN_TILE
    return pl.pallas_call(
        _ffn_ln_kernel,
        grid=(n_tok // tm,),
        in_specs=[
            pl.BlockSpec((tm, D_MODEL), lambda t: (t, 0)),
            _resident((D_MODEL, D_FF)),
            _resident((D_MODEL, D_FF)),
            _resident((D_FF, D_MODEL)),
            _resident((1, D_MODEL)),
            _resident((1, D_MODEL)),
        ],
        out_specs=pl.BlockSpec((tm, D_MODEL), lambda t: (t, 0)),
        out_shape=jax.ShapeDtypeStruct((n_tok, D_MODEL), F32),
        compiler_params=pltpu.CompilerParams(
            dimension_semantics=("parallel",), vmem_limit_bytes=VMEM_LIMIT_BYTES),
        name="ffn_ln",
    )(x2d, wg, wu, wd, g, b)


def _mixer_in_kernel(h_ref, wu_ref, wv_ref, wq_ref, wk_ref, wvm_ref, wga_ref, lng_ref, lnb_ref,
                     ws_ref, bs_ref, wpa_ref, pa_ref, q_ref, k_ref, vt_ref, ya_ref):
    tm = h_ref.shape[1]
    n_chunks = tm // SG_CHUNK
    hb = h_ref[0].astype(BF16)

    u = jax.nn.gelu(_dot(hb, wu_ref[...]))
    v = _layer_norm(jax.nn.gelu(_dot(hb, wv_ref[...])), lng_ref[...], lnb_ref[...]).astype(BF16)
    row = lax.broadcasted_iota(jnp.int32, (SG_CHUNK, SG_CHUNK), 0)
    col = lax.broadcasted_iota(jnp.int32, (SG_CHUNK, SG_CHUNK), 1)
    causal = col <= row
    for g in range(SG_GROUPS):
        cs = slice(g * SG_CHUNK, (g + 1) * SG_CHUNK)
        w = jnp.where(causal, ws_ref[g], 0.0).astype(BF16)
        rhs = jnp.concatenate([v[t * SG_CHUNK:(t + 1) * SG_CHUNK, cs] for t in range(n_chunks)], axis=1)
        mixed = _dot(w, rhs)
        for t in range(n_chunks):
            ts = slice(t * SG_CHUNK, (t + 1) * SG_CHUNK)
            ya_ref[ts, cs] = (u[ts, cs] * (mixed[:, ts] + bs_ref[g])).astype(BF16)

    gate_a = jax.nn.sigmoid(_dot(hb, wga_ref[...]))
    pa_ref[0] = (gate_a * _dot(ya_ref[...], wpa_ref[...])).astype(BF16)

    zq = _dot(hb, wq_ref[...]) * Q_SCALE
    zk = _dot(hb, wk_ref[...])
    zv = _dot(hb, wvm_ref[...])
    pad_row = lax.broadcasted_iota(jnp.int32, (VT_ROWS - MOBA_HEAD_DIM, tm), 0)
    ones_rows = jnp.where(pad_row == 0, 1.0, 0.0).astype(BF16)
    for h in range(MOBA_HEADS):
        hs = slice(h * MOBA_HEAD_DIM, (h + 1) * MOBA_HEAD_DIM)
        q_ref[0, h] = zq[:, hs].astype(BF16)
        k_ref[0, h] = zk[:, hs].astype(BF16)
        vt_ref[0, h, 0:MOBA_HEAD_DIM, :] = zv[:, hs].T.astype(BF16)
        vt_ref[0, h, MOBA_HEAD_DIM:VT_ROWS, :] = ones_rows


def _mixer_in(h1, w_u, w_v, w_q, w_k, w_vm, w_ga, ln_g, ln_b, w_s, b_s_bcast, w_pa):
    bsz, seq, _ = h1.shape
    tm = MIXER_TOKEN_TILE
    sq = jax.ShapeDtypeStruct
    wspec = _resident((D_MODEL, D_MODEL))
    return pl.pallas_call(
        _mixer_in_kernel,
        grid=(bsz, seq // tm),
        in_specs=[
            pl.BlockSpec((1, tm, D_MODEL), lambda b, t: (b, t, 0)),
            wspec, wspec, wspec, wspec, wspec, wspec,
            _resident((1, SG_WIDTH)),
            _resident((1, SG_WIDTH)),
            _resident((SG_GROUPS, SG_CHUNK, SG_CHUNK)),
            _resident((SG_GROUPS, SG_CHUNK, SG_CHUNK)),
            wspec,
        ],
        out_specs=[
            pl.BlockSpec((1, tm, D_MODEL), lambda b, t: (b, t, 0)),
            pl.BlockSpec((1, MOBA_HEADS, tm, MOBA_HEAD_DIM), lambda b, t: (b, 0, t, 0)),
            pl.BlockSpec((1, MOBA_HEADS, tm, MOBA_HEAD_DIM), lambda b, t: (b, 0, t, 0)),
            pl.BlockSpec((1, MOBA_HEADS, VT_ROWS, tm), lambda b, t: (b, 0, 0, t)),
        ],
        out_shape=[
            sq((bsz, seq, D_MODEL), BF16),
            sq((bsz, MOBA_HEADS, seq, MOBA_HEAD_DIM), BF16),
            sq((bsz, MOBA_HEADS, seq, MOBA_HEAD_DIM), BF16),
            sq((bsz, MOBA_HEADS, VT_ROWS, seq), BF16),
        ],
        scratch_shapes=[pltpu.VMEM((tm, SG_WIDTH), BF16)],
        compiler_params=pltpu.CompilerParams(
            dimension_semantics=("parallel", "parallel"), vmem_limit_bytes=VMEM_LIMIT_BYTES),
        name="mixer_in",
    )(h1, w_u, w_v, w_q, w_k, w_vm, w_ga, ln_g, ln_b, w_s, b_s_bcast, w_pa)


_NT = (((1,), (1,)), ((), ()))


def _nt_dot(a, b):
    return lax.dot_general(a, b, _NT, preferred_element_type=F32)


def _moba_kernel(q_ref, k_ref, vt_ref, bias_ref, cfar_ref, o_ref):
    seq, hd = k_ref.shape[2], k_ref.shape[3]
    blk = MOBA_BLOCK
    n_blocks = seq // blk
    rows8 = blk // 8

    km = jnp.sum(k_ref[0, 0].astype(F32).reshape(n_blocks, blk, hd), axis=1) * (1.0 / blk)
    km_hi = km.astype(BF16)
    km_cat = jnp.concatenate([km_hi, (km - km_hi.astype(F32)).astype(BF16)], axis=0)
    blk_id = lax.broadcasted_iota(jnp.int32, (n_blocks, blk), 0)
    cfar = cfar_ref[0]

    def key_rows(n):
        return slice(n * blk, (n + 1) * blk)

    def select(i, q):
        if i <= MOBA_TOPK:
            return [cfar] * max(i - 1, 0), None
        g2 = _nt_dot(km_cat, q)
        gate = jnp.where(blk_id < i, g2[:n_blocks] + g2[n_blocks:], NEG_INF)
        rows = []
        for n in range(i):
            gn = gate[n:n + 1]
            beats = jnp.where(gate > gn, 1.0, jnp.where((gate == gn) & (blk_id < n), 1.0, 0.0))
            rank = jnp.sum(beats, axis=0, keepdims=True)
            rows.append(jnp.where(rank < float(MOBA_TOPK), cfar if n < i - 1 else 0.0, NEG_INF))
        return rows[:-1], rows[-1]

    def begin(i):
        q = q_ref[0, 0, key_rows(i), :]
        far, near = select(i, q)
        return dict(q=q, far=far, near=near, s={}, m8=None, acc=None)

    def scores(i, n, st):
        sn = _nt_dot(k_ref[0, 0, key_rows(n), :], st["q"])
        if n == i:
            sn = sn + bias_ref[0, 0]
        elif n == i - 1:
            sn = sn + bias_ref[0, 1]
            if st["near"] is not None:
                sn = sn + st["near"]
        else:
            sn = sn + st["far"][n]
        st["s"][n] = sn
        m8 = jnp.max(sn.reshape(rows8, 8, blk), axis=0)
        st["m8"] = m8 if st["m8"] is None else jnp.maximum(st["m8"], m8)

    def weigh(i, n, st):
        if "m" not in st:
            st["m"] = jnp.max(st["m8"], axis=0, keepdims=True)
        p = jnp.exp2(st["s"].pop(n) - st["m"])
        pv = _dot(vt_ref[0, 0, :, key_rows(n)], p.astype(BF16))
        st["acc"] = pv if st["acc"] is None else st["acc"] + pv

    def finish(i, st):
        acc = st["acc"]
        inv_l = 1.0 / acc[hd:hd + 1, :]
        o_ref[0, key_rows(i), :] = (acc[0:hd, :] * inv_l).T.astype(BF16)

    ahead = 2
    states = {}
    for i in range(min(ahead, n_blocks)):
        states[i] = begin(i)
        for n in range(i + 1):
            scores(i, n, states[i])
    for i in range(n_blocks):
        j = i + ahead
        if j < n_blocks:
            states[j] = begin(j)
        for n in range(max(i, j if j < n_blocks else 0) + 1):
            if j < n_blocks and n <= j:
                scores(j, n, states[j])
            if n <= i:
                weigh(i, n, states[i])
        finish(i, states.pop(i))


def _moba(q, k, vt, bias_tiles, cfar):
    bsz, n_heads, seq, hd = q.shape
    head_major = lambda b, h: (b, h, 0, 0)
    return pl.pallas_call(
        _moba_kernel,
        grid=(bsz, n_heads),
        in_specs=[
            pl.BlockSpec((1, 1, seq, hd), head_major),
            pl.BlockSpec((1, 1, seq, hd), head_major),
            pl.BlockSpec((1, 1, VT_ROWS, seq), head_major),
            pl.BlockSpec((1, 2, MOBA_BLOCK, MOBA_BLOCK), lambda b, h: (h, 0, 0, 0)),
            pl.BlockSpec((1, 1, MOBA_BLOCK), lambda b, h: (h, 0, 0)),
        ],
        out_specs=pl.BlockSpec((1, seq, hd), lambda b, h: (b, 0, h)),
        out_shape=jax.ShapeDtypeStruct((bsz, seq, n_heads * hd), BF16),
        compiler_params=pltpu.CompilerParams(
            dimension_semantics=("parallel", "parallel"), vmem_limit_bytes=VMEM_LIMIT_BYTES),
        name="moba",
    )(q, k, vt, bias_tiles, cfar)


def _rel_bucket_table():
    n = np.arange(2 * MOBA_BLOCK, dtype=np.int32)
    max_exact = REL_BUCKETS // 2
    nf = np.maximum(n, 1).astype(np.float32)
    ratio = np.log(nf / np.float32(max_exact)) / np.float32(math.log(REL_MAX_DIST / max_exact))
    large = max_exact + (ratio * np.float32(REL_BUCKETS - max_exact)).astype(np.int32)
    large = np.minimum(large, REL_BUCKETS - 1)
    return np.where(n < max_exact, n, large).astype(np.int32)


def _rel_bias_tiles(rel_table):
    bucket = _rel_bucket_table()
    assert np.all(np.diff(bucket) >= 0) and int(bucket[MOBA_BLOCK + 1]) == REL_BUCKETS - 1
    key = np.arange(MOBA_BLOCK)[:, None]
    qry = np.arange(MOBA_BLOCK)[None, :]
    dist = jnp.asarray(np.stack([qry - key, qry - key + MOBA_BLOCK]))[None]
    table_t = (rel_table.T.astype(F32) * LOG2E)[:, :, None, None, None]
    tiles = jnp.broadcast_to(table_t[:, 0], (MOBA_HEADS, 2, MOBA_BLOCK, MOBA_BLOCK))
    for d in range(1, 2 * MOBA_BLOCK):
        if bucket[d] != bucket[d - 1]:
            tiles = jnp.where(dist >= d, table_t[:, int(bucket[d])], tiles)
    tiles = jnp.where(dist >= 0, tiles, NEG_INF)
    cfar = jnp.broadcast_to(table_t[:, REL_BUCKETS - 1, 0], (MOBA_HEADS, 1, MOBA_BLOCK))
    return tiles, cfar


def _merge_ffn_kernel(h_ref, pa_ref, yb_ref, wgb_ref, wpb_ref, wout_ref, g2_ref, b2_ref,
                      wg_ref, wu_ref, wd_ref, g3_ref, b3_ref, o_ref):
    h = h_ref[...]
    gate_b = jax.nn.sigmoid(_dot(h.astype(BF16), wgb_ref[...]))
    merged = pa_ref[...].astype(F32) + gate_b * _dot(yb_ref[...], wpb_ref[...])
    mixed = _dot(merged.astype(BF16), wout_ref[...])
    h2 = _layer_norm(DEEPNORM_ALPHA * h + mixed, g2_ref[...], b2_ref[...])
    f = _swiglu(h2.astype(BF16), wg_ref, wu_ref, wd_ref)
    o_ref[...] = _layer_norm(DEEPNORM_ALPHA * h2 + MACARON_WEIGHT * f, g3_ref[...], b3_ref[...])


def _merge_ffn(h1, pa, yb, w_gb, w_pb, w_out, g2, b2, wg, wu, wd, g3, b3):
    n_tok = h1.shape[0]
    tm = MERGE_TOKEN_TILE
    tok = lambda: pl.BlockSpec((tm, D_MODEL), lambda t: (t, 0))
    wspec = _resident((D_MODEL, D_MODEL))
    vec = _resident((1, D_MODEL))
    return pl.pallas_call(
        _merge_ffn_kernel,
        grid=(n_tok // tm,),
        in_specs=[tok(), tok(), tok(), wspec, wspec, wspec, vec, vec,
                  _resident((D_MODEL, D_FF)), _resident((D_MODEL, D_FF)), _resident((D_FF, D_MODEL)),
                  vec, vec],
        out_specs=tok(),
        out_shape=jax.ShapeDtypeStruct((n_tok, D_MODEL), F32),
        compiler_params=pltpu.CompilerParams(
            dimension_semantics=("parallel",), vmem_limit_bytes=VMEM_LIMIT_BYTES),
        name="merge_ffn",
    )(h1, pa, yb, w_gb, w_pb, w_out, g2, b2, wg, wu, wd, g3, b3)


def kernel(x, ffn1_w_gate, ffn1_w_up, ffn1_w_down, ln1_g, ln1_b, w_in, sg_ln_g, sg_ln_b, sg_w_s, sg_b_s,
           rel_table, w_proj_a, w_proj_b, w_out, ln2_g, ln2_b, ffn2_w_gate, ffn2_w_up, ffn2_w_down,
           ln3_g, ln3_b):
    bsz, seq, d = x.shape
    n_tok = bsz * seq
    bias_tiles, cfar = _rel_bias_tiles(rel_table)
    vec = lambda a: a.reshape(1, -1).astype(F32)
    h = x.reshape(n_tok, d)
    for l in range(DEPTH):
        h = _ffn_ln(h, ffn1_w_gate[l].astype(BF16), ffn1_w_up[l].astype(BF16),
                    ffn1_w_down[l].astype(BF16), vec(ln1_g[l]), vec(ln1_b[l]))
        wi = w_in[l].astype(BF16)
        cols = [wi[:, c * D_MODEL:(c + 1) * D_MODEL] for c in range(7)]
        b_s_bcast = jnp.broadcast_to(sg_b_s[l][:, :, None], (SG_GROUPS, SG_CHUNK, SG_CHUNK)).astype(F32)
        pa, q, k, vt = _mixer_in(h.reshape(bsz, seq, d), cols[0], cols[1], cols[2], cols[3], cols[4],
                                 cols[5], vec(sg_ln_g[l]), vec(sg_ln_b[l]), sg_w_s[l].astype(F32),
                                 b_s_bcast, w_proj_a[l].astype(BF16))
        yb = _moba(q, k, vt, bias_tiles, cfar)
        h = _merge_ffn(h, pa.reshape(n_tok, d), yb.reshape(n_tok, d), cols[6], w_proj_b[l].astype(BF16),
                       w_out[l].astype(BF16), vec(ln2_g[l]), vec(ln2_b[l]),
                       ffn2_w_gate[l].astype(BF16), ffn2_w_up[l].astype(BF16),
                       ffn2_w_down[l].astype(BF16), vec(ln3_g[l]), vec(ln3_b[l]))
    return h.reshape(bsz, seq, d)
```

```python
import functools
import math

import jax
import jax.numpy as jnp
import numpy as np
from jax import lax
from jax.experimental import pallas as pl
from jax.experimental.pallas import tpu as pltpu

D_MODEL = 1024
D_FF = 2816
DEPTH = 1
SG_GROUPS = 8
SG_CHUNK = 128
SG_WIDTH = 1024
MOBA_HEADS = 8
MOBA_HEAD_DIM = 128
MOBA_WIDTH = 1024
MOBA_BLOCK = 256
MOBA_TOPK = 3
REL_BUCKETS = 32
REL_MAX_DIST = 128
MACARON_WEIGHT = 0.5
DEEPNORM_ALPHA = (2.0 * DEPTH) ** 0.25
LN_EPS = 1e-5
NEG_INF = -1e30
LOG2E = math.log2(math.e)
Q_SCALE = MOBA_HEAD_DIM ** -0.5 * LOG2E

VMEM_LIMIT_BYTES = 56 * 1024 * 1024

F32 = jnp.float32
BF16 = jnp.bfloat16

FFN_TOKEN_TILE = 512
MIXER_TOKEN_TILE = 512
MERGE_TOKEN_TILE = 512
ROW_GROUP = 256
VT_ROWS = MOBA_HEAD_DIM + 16


def _layer_norm(y, g, b):
    mu = jnp.mean(y, axis=-1, keepdims=True)
    d = y - mu
    var = jnp.mean(d * d, axis=-1, keepdims=True)
    return d * lax.rsqrt(var + LN_EPS) * g + b


def _dot(a, b):
    return jnp.dot(a, b, preferred_element_type=F32)


def _swiglu(xb, wg_ref, wu_ref, wd_ref):
    gate = _dot(xb, wg_ref[...])
    up = _dot(xb, wu_ref[...])
    act = (gate * jax.nn.sigmoid(gate) * up).astype(BF16)
    return _dot(act, wd_ref[...])


def _resident(shape):
    zeros = (0,) * len(shape)
    return pl.BlockSpec(shape, lambda *_: zeros, pipeline_mode=pl.Buffered(1))


def _row_groups(n_rows):
    return [slice(r, r + ROW_GROUP) for r in range(0, n_rows, ROW_GROUP)]


def _ffn_ln_kernel(x_ref, wg_ref, wu_ref, wd_ref, g_ref, b_ref, o_ref):
    groups = _row_groups(x_ref.shape[0])
    xs = [x_ref[rs, :] for rs in groups]
    fs = [_swiglu(x.astype(BF16), wg_ref, wu_ref, wd_ref) for x in xs]
    for rs, x, f in zip(groups, xs, fs):
        o_ref[rs, :] = _layer_norm(DEEPNORM_ALPHA * x + MACARON_WEIGHT * f, g_ref[...], b_ref[...])


def _ffn_ln(x2d, wg, wu, wd, g, b):
    n_tok = x2d.shape[0]
    tm = FFN_TOKEN_TILE
    return pl.pallas_call(
        _ffn_ln_kernel,
        grid=(n_tok // tm,),
        in_specs=[
            pl.BlockSpec((tm, D_MODEL), lambda t: (t, 0)),
            _resident((D_MODEL, D_FF)),
            _resident((D_MODEL, D_FF)),
            _resident((D_FF, D_MODEL)),
            _resident((1, D_MODEL)),
            _resident((1, D_MODEL)),
        ],
        out_specs=pl.BlockSpec((tm, D_MODEL), lambda t: (t, 0)),
        out_shape=jax.ShapeDtypeStruct((n_tok, D_MODEL), F32),
        compiler_params=pltpu.CompilerParams(
            dimension_semantics=("parallel",), vmem_limit_bytes=VMEM_LIMIT_BYTES),
        name="ffn_ln",
    )(x2d, wg, wu, wd, g, b)


def _mixer_in_kernel(h_ref, wu_ref, wv_ref, wq_ref, wk_ref, wvm_ref, wga_ref, lng_ref, lnb_ref,
                     ws_ref, bs_ref, wpa_ref, pa_ref, q_ref, k_ref, vt_ref, ya_ref):
    tm = h_ref.shape[1]
    n_chunks = tm // SG_CHUNK
    hb = h_ref[0].astype(BF16)

    zu = _dot(hb, wu_ref[...])
    zv_gate = _dot(hb, wv_ref[...])
    zq = _dot(hb, wq_ref[...]) * Q_SCALE
    zk = _dot(hb, wk_ref[...])
    zv = _dot(hb, wvm_ref[...])
    z_ga = _dot(hb, wga_ref[...])

    u = jax.nn.gelu(zu)
    v = _layer_norm(jax.nn.gelu(zv_gate), lng_ref[...], lnb_ref[...]).astype(BF16)
    row = lax.broadcasted_iota(jnp.int32, (SG_CHUNK, SG_CHUNK), 0)
    col = lax.broadcasted_iota(jnp.int32, (SG_CHUNK, SG_CHUNK), 1)
    causal = col <= row
    for g in range(SG_GROUPS):
        cs = slice(g * SG_CHUNK, (g + 1) * SG_CHUNK)
        w = jnp.where(causal, ws_ref[g], 0.0).astype(BF16)
        rhs = jnp.concatenate([v[t * SG_CHUNK:(t + 1) * SG_CHUNK, cs] for t in range(n_chunks)], axis=1)
        mixed = _dot(w, rhs)
        for t in range(n_chunks):
            ts = slice(t * SG_CHUNK, (t + 1) * SG_CHUNK)
            ya_ref[ts, cs] = (u[ts, cs] * (mixed[:, ts] + bs_ref[g])).astype(BF16)

    pa_ref[0] = (jax.nn.sigmoid(z_ga) * _dot(ya_ref[...], wpa_ref[...])).astype(BF16)

    pad_row = lax.broadcasted_iota(jnp.int32, (VT_ROWS - MOBA_HEAD_DIM, tm), 0)
    ones_rows = jnp.where(pad_row == 0, 1.0, 0.0).astype(BF16)
    for h in range(MOBA_HEADS):
        hs = slice(h * MOBA_HEAD_DIM, (h + 1) * MOBA_HEAD_DIM)
        q_ref[0, h] = zq[:, hs].astype(BF16)
        k_ref[0, h] = zk[:, hs].astype(BF16)
        vt_ref[0, h, 0:MOBA_HEAD_DIM, :] = zv[:, hs].T.astype(BF16)
        vt_ref[0, h, MOBA_HEAD_DIM:VT_ROWS, :] = ones_rows


def _mixer_in(h1, w_u, w_v, w_q, w_k, w_vm, w_ga, ln_g, ln_b, w_s, b_s_bcast, w_pa):
    bsz, seq, _ = h1.shape
    tm = MIXER_TOKEN_TILE
    sq = jax.ShapeDtypeStruct
    wspec = _resident((D_MODEL, D_MODEL))
    return pl.pallas_call(
        _mixer_in_kernel,
        grid=(bsz, seq // tm),
        in_specs=[
            pl.BlockSpec((1, tm, D_MODEL), lambda b, t: (b, t, 0)),
            wspec, wspec, wspec, wspec, wspec, wspec,
            _resident((1, SG_WIDTH)),
            _resident((1, SG_WIDTH)),
            _resident((SG_GROUPS, SG_CHUNK, SG_CHUNK)),
            _resident((SG_GROUPS, SG_CHUNK, SG_CHUNK)),
            wspec,
        ],
        out_specs=[
            pl.BlockSpec((1, tm, D_MODEL), lambda b, t: (b, t, 0)),
            pl.BlockSpec((1, MOBA_HEADS, tm, MOBA_HEAD_DIM), lambda b, t: (b, 0, t, 0)),
            pl.BlockSpec((1, MOBA_HEADS, tm, MOBA_HEAD_DIM), lambda b, t: (b, 0, t, 0)),
            pl.BlockSpec((1, MOBA_HEADS, VT_ROWS, tm), lambda b, t: (b, 0, 0, t)),
        ],
        out_shape=[
            sq((bsz, seq, D_MODEL), BF16),
            sq((bsz, MOBA_HEADS, seq, MOBA_HEAD_DIM), BF16),
            sq((bsz, MOBA_HEADS, seq, MOBA_HEAD_DIM), BF16),
            sq((bsz, MOBA_HEADS, VT_ROWS, seq), BF16),
        ],
        scratch_shapes=[pltpu.VMEM((tm, SG_WIDTH), BF16)],
        compiler_params=pltpu.CompilerParams(
            dimension_semantics=("parallel", "parallel"), vmem_limit_bytes=VMEM_LIMIT_BYTES),
        name="mixer_in",
    )(h1, w_u, w_v, w_q, w_k, w_vm, w_ga, ln_g, ln_b, w_s, b_s_bcast, w_pa)


_NT = (((1,), (1,)), ((), ()))


def _nt_dot(a, b):
    return lax.dot_general(a, b, _NT, preferred_element_type=F32)


def _moba_kernel(q_ref, k_ref, vt_ref, bias_ref, cfar_ref, o_ref):
    seq, hd = k_ref.shape[2], k_ref.shape[3]
    blk = MOBA_BLOCK
    n_blocks = seq // blk
    rows8 = blk // 8

    km = jnp.sum(k_ref[0, 0].astype(F32).reshape(n_blocks, blk, hd), axis=1) * (1.0 / blk)
    km_hi = km.astype(BF16)
    km_cat = jnp.concatenate([km_hi, (km - km_hi.astype(F32)).astype(BF16)], axis=0)
    blk_id = lax.broadcasted_iota(jnp.int32, (n_blocks, blk), 0)
    cfar = cfar_ref[0]

    def key_rows(n):
        return slice(n * blk, (n + 1) * blk)

    def select(i, q):
        if i <= MOBA_TOPK:
            return [cfar] * max(i - 1, 0), None
        g2 = _nt_dot(km_cat, q)
        gate = jnp.where(blk_id < i, g2[:n_blocks] + g2[n_blocks:], NEG_INF)
        rows = []
        for n in range(i):
            gn = gate[n:n + 1]
            beats = jnp.where(gate > gn, 1.0, jnp.where((gate == gn) & (blk_id < n), 1.0, 0.0))
            rank = jnp.sum(beats, axis=0, keepdims=True)
            rows.append(jnp.where(rank < float(MOBA_TOPK), cfar if n < i - 1 else 0.0, NEG_INF))
        return rows[:-1], rows[-1]

    def begin(i):
        q = q_ref[0, 0, key_rows(i), :]
        far, near = select(i, q)
        return dict(q=q, far=far, near=near, s={}, m8=None, acc=None)

    def scores(i, n, st):
        sn = _nt_dot(k_ref[0, 0, key_rows(n), :], st["q"])
        if n == i:
            sn = sn + bias_ref[0, 0]
        elif n == i - 1:
            sn = sn + bias_ref[0, 1]
            if st["near"] is not None:
                sn = sn + st["near"]
        else:
            sn = sn + st["far"][n]
        st["s"][n] = sn
        m8 = jnp.max(sn.reshape(rows8, 8, blk), axis=0)
        st["m8"] = m8 if st["m8"] is None else jnp.maximum(st["m8"], m8)

    def weigh(i, n, st):
        if "m" not in st:
            st["m"] = jnp.max(st["m8"], axis=0, keepdims=True)
        p = jnp.exp2(st["s"].pop(n) - st["m"])
        pv = _dot(vt_ref[0, 0, :, key_rows(n)], p.astype(BF16))
        st["acc"] = pv if st["acc"] is None else st["acc"] + pv

    def finish(i, st):
        acc = st["acc"]
        inv_l = 1.0 / acc[hd:hd + 1, :]
        o_ref[0, key_rows(i), :] = (acc[0:hd, :] * inv_l).T.astype(BF16)

    ahead = 2
    states = {}
    for i in range(min(ahead, n_blocks)):
        states[i] = begin(i)
        for n in range(i + 1):
            scores(i, n, states[i])
    for i in range(n_blocks):
        j = i + ahead
        if j < n_blocks:
            states[j] = begin(j)
        for n in range(max(i, j if j < n_blocks else 0) + 1):
            if j < n_blocks and n <= j:
                scores(j, n, states[j])
            if n <= i:
                weigh(i, n, states[i])
        finish(i, states.pop(i))


def _moba(q, k, vt, bias_tiles, cfar):
    bsz, n_heads, seq, hd = q.shape
    head_major = lambda b, h: (b, h, 0, 0)
    return pl.pallas_call(
        _moba_kernel,
        grid=(bsz, n_heads),
        in_specs=[
            pl.BlockSpec((1, 1, seq, hd), head_major),
            pl.BlockSpec((1, 1, seq, hd), head_major),
            pl.BlockSpec((1, 1, VT_ROWS, seq), head_major),
            pl.BlockSpec((1, 2, MOBA_BLOCK, MOBA_BLOCK), lambda b, h: (h, 0, 0, 0)),
            pl.BlockSpec((1, 1, MOBA_BLOCK), lambda b, h: (h, 0, 0)),
        ],
        out_specs=pl.BlockSpec((1, seq, hd), lambda b, h: (b, 0, h)),
        out_shape=jax.ShapeDtypeStruct((bsz, seq, n_heads * hd), BF16),
        compiler_params=pltpu.CompilerParams(
            dimension_semantics=("parallel", "parallel"), vmem_limit_bytes=VMEM_LIMIT_BYTES),
        name="moba",
    )(q, k, vt, bias_tiles, cfar)


def _rel_bucket_table():
    n = np.arange(2 * MOBA_BLOCK, dtype=np.int32)
    max_exact = REL_BUCKETS // 2
    nf = np.maximum(n, 1).astype(np.float32)
    ratio = np.log(nf / np.float32(max_exact)) / np.float32(math.log(REL_MAX_DIST / max_exact))
    large = max_exact + (ratio * np.float32(REL_BUCKETS - max_exact)).astype(np.int32)
    large = np.minimum(large, REL_BUCKETS - 1)
    return np.where(n < max_exact, n, large).astype(np.int32)


def _rel_bias_tiles(rel_table):
    bucket = _rel_bucket_table()
    assert np.all(np.diff(bucket) >= 0) and int(bucket[MOBA_BLOCK + 1]) == REL_BUCKETS - 1
    key = np.arange(MOBA_BLOCK)[:, None]
    qry = np.arange(MOBA_BLOCK)[None, :]
    dist = jnp.asarray(np.stack([qry - key, qry - key + MOBA_BLOCK]))[None]
    table_t = (rel_table.T.astype(F32) * LOG2E)[:, :, None, None, None]
    tiles = jnp.broadcast_to(table_t[:, 0], (MOBA_HEADS, 2, MOBA_BLOCK, MOBA_BLOCK))
    for d in range(1, 2 * MOBA_BLOCK):
        if bucket[d] != bucket[d - 1]:
            tiles = jnp.where(dist >= d, table_t[:, int(bucket[d])], tiles)
    tiles = jnp.where(dist >= 0, tiles, NEG_INF)
    cfar = jnp.broadcast_to(table_t[:, REL_BUCKETS - 1, 0], (MOBA_HEADS, 1, MOBA_BLOCK))
    return tiles, cfar


def _merge_ffn_kernel(h_ref, pa_ref, yb_ref, wgb_ref, wpb_ref, wout_ref, g2_ref, b2_ref,
                      wg_ref, wu_ref, wd_ref, g3_ref, b3_ref, o_ref):
    groups = _row_groups(h_ref.shape[0])
    hs = [h_ref[rs, :] for rs in groups]
    mixed = []
    for rs, h in zip(groups, hs):
        gate_b = jax.nn.sigmoid(_dot(h.astype(BF16), wgb_ref[...]))
        merged = pa_ref[rs, :].astype(F32) + gate_b * _dot(yb_ref[rs, :], wpb_ref[...])
        mixed.append(_dot(merged.astype(BF16), wout_ref[...]))
    h2s = [_layer_norm(DEEPNORM_ALPHA * h + m, g2_ref[...], b2_ref[...]) for h, m in zip(hs, mixed)]
    fs = [_swiglu(h2.astype(BF16), wg_ref, wu_ref, wd_ref) for h2 in h2s]
    for rs, h2, f in zip(groups, h2s, fs):
        o_ref[rs, :] = _layer_norm(DEEPNORM_ALPHA * h2 + MACARON_WEIGHT * f, g3_ref[...], b3_ref[...])


def _merge_ffn(h1, pa, yb, w_gb, w_pb, w_out, g2, b2, wg, wu, wd, g3, b3):
    n_tok = h1.shape[0]
    tm = MERGE_TOKEN_TILE
    tok = lambda: pl.BlockSpec((tm, D_MODEL), lambda t: (t, 0))
    wspec = _resident((D_MODEL, D_MODEL))
    vec = _resident((1, D_MODEL))
    return pl.pallas_call(
        _merge_ffn_kernel,
        grid=(n_tok // tm,),
        in_specs=[tok(), tok(), tok(), wspec, wspec, wspec, vec, vec,
                  _resident((D_MODEL, D_FF)), _resident((D_MODEL, D_FF)), _resident((D_FF, D_MODEL)),
                  vec, vec],
        out_specs=tok(),
        out_shape=jax.ShapeDtypeStruct((n_tok, D_MODEL), F32),
        compiler_params=pltpu.CompilerParams(
            dimension_semantics=("parallel",), vmem_limit_bytes=VMEM_LIMIT_BYTES),
        name="merge_ffn",
    )(h1, pa, yb, w_gb, w_pb, w_out, g2, b2, wg, wu, wd, g3, b3)


def kernel(x, ffn1_w_gate, ffn1_w_up, ffn1_w_down, ln1_g, ln1_b, w_in, sg_ln_g, sg_ln_b, sg_w_s, sg_b_s,
           rel_table, w_proj_a, w_proj_b, w_out, ln2_g, ln2_b, ffn2_w_gate, ffn2_w_up, ffn2_w_down,
           ln3_g, ln3_b):
    bsz, seq, d = x.shape
    n_tok = bsz * seq
    bias_tiles, cfar = _rel_bias_tiles(rel_table)
    vec = lambda a: a.reshape(1, -1).astype(F32)
    h = x.reshape(n_tok, d)
    for l in range(DEPTH):
        h = _ffn_ln(h, ffn1_w_gate[l].astype(BF16), ffn1_w_up[l].astype(BF16),
                    ffn1_w_down[l].astype(BF16), vec(ln1_g[l]), vec(ln1_b[l]))
        wi = w_in[l].astype(BF16)
        cols = [wi[:, c * D_MODEL:(c + 1) * D_MODEL] for c in range(7)]
        b_s_bcast = jnp.broadcast_to(sg_b_s[l][:, :, None], (SG_GROUPS, SG_CHUNK, SG_CHUNK)).astype(F32)
        pa, q, k, vt = _mixer_in(h.reshape(bsz, seq, d), cols[0], cols[1], cols[2], cols[3], cols[4],
                                 cols[5], vec(sg_ln_g[l]), vec(sg_ln_b[l]), sg_w_s[l].astype(F32),
                                 b_s_bcast, w_proj_a[l].astype(BF16))
        yb = _moba(q, k, vt, bias_tiles, cfar)
        h = _merge_ffn(h, pa.reshape(n_tok, d), yb.reshape(n_tok, d), cols[6], w_proj_b[l].astype(BF16),
                       w_out[l].astype(BF16), vec(ln2_g[l]), vec(ln2_b[l]),
                       ffn2_w_gate[l].astype(BF16), ffn2_w_up[l].astype(BF16),
                       ffn2_w_down[l].astype(BF16), vec(ln3_g[l]), vec(ln3_b[l]))
    return h.reshape(bsz, seq, d)
```

```python
import functools
import math

import jax
import jax.numpy as jnp
import numpy as np
from jax import lax
from jax.experimental import pallas as pl
from jax.experimental.pallas import tpu as pltpu

D_MODEL = 1024
D_FF = 2816
DEPTH = 1
SG_GROUPS = 8
SG_CHUNK = 128
SG_WIDTH = 1024
MOBA_HEADS = 8
MOBA_HEAD_DIM = 128
MOBA_WIDTH = 1024
MOBA_BLOCK = 256
MOBA_TOPK = 3
REL_BUCKETS = 32
REL_MAX_DIST = 128
MACARON_WEIGHT = 0.5
DEEPNORM_ALPHA = (2.0 * DEPTH) ** 0.25
LN_EPS = 1e-5
NEG_INF = -1e30
LOG2E = math.log2(math.e)
Q_SCALE = MOBA_HEAD_DIM ** -0.5 * LOG2E

VMEM_LIMIT_BYTES = 56 * 1024 * 1024

F32 = jnp.float32
BF16 = jnp.bfloat16

FFN_TOKEN_TILE = 1024
MIXER_TOKEN_TILE = 512
MERGE_TOKEN_TILE = 512
ROW_GROUP = 256
VT_ROWS = MOBA_HEAD_DIM + 16


def _layer_norm(y, g, b):
    mu = jnp.mean(y, axis=-1, keepdims=True)
    d = y - mu
    var = jnp.mean(d * d, axis=-1, keepdims=True)
    return d * lax.rsqrt(var + LN_EPS) * g + b


def _dot(a, b):
    return jnp.dot(a, b, preferred_element_type=F32)


def _swiglu(xb, wg_ref, wu_ref, wd_ref):
    gate = _dot(xb, wg_ref[...])
    up = _dot(xb, wu_ref[...])
    act = (gate * jax.nn.sigmoid(gate) * up).astype(BF16)
    return _dot(act, wd_ref[...])


def _resident(shape):
    zeros = (0,) * len(shape)
    return pl.BlockSpec(shape, lambda *_: zeros, pipeline_mode=pl.Buffered(1))


def _row_groups(n_rows):
    return [slice(r, r + ROW_GROUP) for r in range(0, n_rows, ROW_GROUP)]


def _ffn_ln_kernel(x_ref, wg_ref, wu_ref, wd_ref, g_ref, b_ref, o_ref):
    groups = _row_groups(x_ref.shape[0])
    xs = [x_ref[rs, :] for rs in groups]
    fs = [_swiglu(x.astype(BF16), wg_ref, wu_ref, wd_ref) for x in xs]
    for rs, x, f in zip(groups, xs, fs):
        o_ref[rs, :] = _layer_norm(DEEPNORM_ALPHA * x + MACARON_WEIGHT * f, g_ref[...], b_ref[...])


def _ffn_ln(x2d, wg, wu, wd, g, b):
    n_tok = x2d.shape[0]
    tm = FFN_TOKEN_TILE
    return pl.pallas_call(
        _ffn_ln_kernel,
        grid=(n_tok // tm,),
        in_specs=[
            pl.BlockSpec((tm, D_MODEL), lambda t: (t, 0)),
            _resident((D_MODEL, D_FF)),
            _resident((D_MODEL, D_FF)),
            _resident((D_FF, D_MODEL)),
            _resident((1, D_MODEL)),
            _resident((1, D_MODEL)),
        ],
        out_specs=pl.BlockSpec((tm, D_MODEL), lambda t: (t, 0)),
        out_shape=jax.ShapeDtypeStruct((n_tok, D_MODEL), F32),
        compiler_params=pltpu.CompilerParams(
            dimension_semantics=("parallel",), vmem_limit_bytes=VMEM_LIMIT_BYTES),
        name="ffn_ln",
    )(x2d, wg, wu, wd, g, b)


def _mixer_in_kernel(h_ref, wu_ref, wv_ref, wq_ref, wk_ref, wvm_ref, wga_ref, lng_ref, lnb_ref,
                     ws_ref, bs_ref, wpa_ref, pa_ref, q_ref, k_ref, vt_ref, ya_ref):
    tm = h_ref.shape[1]
    n_chunks = tm // SG_CHUNK
    hb = h_ref[0].astype(BF16)

    zu = _dot(hb, wu_ref[...])
    zv_gate = _dot(hb, wv_ref[...])
    zq = _dot(hb, wq_ref[...]) * Q_SCALE
    zk = _dot(hb, wk_ref[...])
    zv = _dot(hb, wvm_ref[...])
    z_ga = _dot(hb, wga_ref[...])

    u = jax.nn.gelu(zu)
    v = _layer_norm(jax.nn.gelu(zv_gate), lng_ref[...], lnb_ref[...]).astype(BF16)
    row = lax.broadcasted_iota(jnp.int32, (SG_CHUNK, SG_CHUNK), 0)
    col = lax.broadcasted_iota(jnp.int32, (SG_CHUNK, SG_CHUNK), 1)
    causal = col <= row
    for g in range(SG_GROUPS):
        cs = slice(g * SG_CHUNK, (g + 1) * SG_CHUNK)
        w = jnp.where(causal, ws_ref[g], 0.0).astype(BF16)
        rhs = jnp.concatenate([v[t * SG_CHUNK:(t + 1) * SG_CHUNK, cs] for t in range(n_chunks)], axis=1)
        mixed = _dot(w, rhs)
        for t in range(n_chunks):
            ts = slice(t * SG_CHUNK, (t + 1) * SG_CHUNK)
            ya_ref[ts, cs] = (u[ts, cs] * (mixed[:, ts] + bs_ref[g])).astype(BF16)

    pa_ref[0] = (jax.nn.sigmoid(z_ga) * _dot(ya_ref[...], wpa_ref[...])).astype(BF16)

    pad_row = lax.broadcasted_iota(jnp.int32, (VT_ROWS - MOBA_HEAD_DIM, tm), 0)
    ones_rows = jnp.where(pad_row == 0, 1.0, 0.0).astype(BF16)
    for h in range(MOBA_HEADS):
        hs = slice(h * MOBA_HEAD_DIM, (h + 1) * MOBA_HEAD_DIM)
        q_ref[0, h] = zq[:, hs].astype(BF16)
        k_ref[0, h] = zk[:, hs].astype(BF16)
        vt_ref[0, h, 0:MOBA_HEAD_DIM, :] = zv[:, hs].T.astype(BF16)
        vt_ref[0, h, MOBA_HEAD_DIM:VT_ROWS, :] = ones_rows


def _mixer_in(h1, w_u, w_v, w_q, w_k, w_vm, w_ga, ln_g, ln_b, w_s, b_s_bcast, w_pa):
    bsz, seq, _ = h1.shape
    tm = MIXER_TOKEN_TILE
    sq = jax.ShapeDtypeStruct
    wspec = _resident((D_MODEL, D_MODEL))
    return pl.pallas_call(
        _mixer_in_kernel,
        grid=(bsz, seq // tm),
        in_specs=[
            pl.BlockSpec((1, tm, D_MODEL), lambda b, t: (b, t, 0)),
            wspec, wspec, wspec, wspec, wspec, wspec,
            _resident((1, SG_WIDTH)),
            _resident((1, SG_WIDTH)),
            _resident((SG_GROUPS, SG_CHUNK, SG_CHUNK)),
            _resident((SG_GROUPS, SG_CHUNK, SG_CHUNK)),
            wspec,
        ],
        out_specs=[
            pl.BlockSpec((1, tm, D_MODEL), lambda b, t: (b, t, 0)),
            pl.BlockSpec((1, MOBA_HEADS, tm, MOBA_HEAD_DIM), lambda b, t: (b, 0, t, 0)),
            pl.BlockSpec((1, MOBA_HEADS, tm, MOBA_HEAD_DIM), lambda b, t: (b, 0, t, 0)),
            pl.BlockSpec((1, MOBA_HEADS, VT_ROWS, tm), lambda b, t: (b, 0, 0, t)),
        ],
        out_shape=[
            sq((bsz, seq, D_MODEL), BF16),
            sq((bsz, MOBA_HEADS, seq, MOBA_HEAD_DIM), BF16),
            sq((bsz, MOBA_HEADS, seq, MOBA_HEAD_DIM), BF16),
            sq((bsz, MOBA_HEADS, VT_ROWS, seq), BF16),
        ],
        scratch_shapes=[pltpu.VMEM((tm, SG_WIDTH), BF16)],
        compiler_params=pltpu.CompilerParams(
            dimension_semantics=("parallel", "parallel"), vmem_limit_bytes=VMEM_LIMIT_BYTES),
        name="mixer_in",
    )(h1, w_u, w_v, w_q, w_k, w_vm, w_ga, ln_g, ln_b, w_s, b_s_bcast, w_pa)


_NT = (((1,), (1,)), ((), ()))


def _nt_dot(a, b):
    return lax.dot_general(a, b, _NT, preferred_element_type=F32)


def _moba_kernel(q_ref, k_ref, vt_ref, bias_ref, cfar_ref, o_ref):
    seq, hd = k_ref.shape[2], k_ref.shape[3]
    blk = MOBA_BLOCK
    n_blocks = seq // blk
    rows8 = blk // 8

    km = jnp.sum(k_ref[0, 0].astype(F32).reshape(n_blocks, blk, hd), axis=1) * (1.0 / blk)
    km_hi = km.astype(BF16)
    km_cat = jnp.concatenate([km_hi, (km - km_hi.astype(F32)).astype(BF16)], axis=0)
    blk_id = lax.broadcasted_iota(jnp.int32, (n_blocks, blk), 0)
    cfar = cfar_ref[0]

    def key_rows(n):
        return slice(n * blk, (n + 1) * blk)

    def select(i, q):
        if i <= MOBA_TOPK:
            return [cfar] * max(i - 1, 0), None
        g2 = _nt_dot(km_cat, q)
        gate = jnp.where(blk_id < i, g2[:n_blocks] + g2[n_blocks:], NEG_INF)
        rows = []
        for n in range(i):
            gn = gate[n:n + 1]
            beats = jnp.where(gate > gn, 1.0, jnp.where((gate == gn) & (blk_id < n), 1.0, 0.0))
            rank = jnp.sum(beats, axis=0, keepdims=True)
            rows.append(jnp.where(rank < float(MOBA_TOPK), cfar if n < i - 1 else 0.0, NEG_INF))
        return rows[:-1], rows[-1]

    def begin(i):
        q = q_ref[0, 0, key_rows(i), :]
        far, near = select(i, q)
        return dict(q=q, far=far, near=near, s={}, m8=None, acc=None)

    def scores(i, n, st):
        sn = _nt_dot(k_ref[0, 0, key_rows(n), :], st["q"])
        if n == i:
            sn = sn + bias_ref[0, 0]
        elif n == i - 1:
            sn = sn + bias_ref[0, 1]
            if st["near"] is not None:
                sn = sn + st["near"]
        else:
            sn = sn + st["far"][n]
        st["s"][n] = sn
        m8 = jnp.max(sn.reshape(rows8, 8, blk), axis=0)
        st["m8"] = m8 if st["m8"] is None else jnp.maximum(st["m8"], m8)

    def weigh(i, n, st):
        if "m" not in st:
            st["m"] = jnp.max(st["m8"], axis=0, keepdims=True)
        p = jnp.exp2(st["s"].pop(n) - st["m"])
        pv = _dot(vt_ref[0, 0, :, key_rows(n)], p.astype(BF16))
        st["acc"] = pv if st["acc"] is None else st["acc"] + pv

    def finish(i, st):
        acc = st["acc"]
        inv_l = 1.0 / acc[hd:hd + 1, :]
        o_ref[0, key_rows(i), :] = (acc[0:hd, :] * inv_l).T.astype(BF16)

    ahead = 2
    order = list(range(n_blocks - 1, -1, -1))
    states = {}
    for i in order[:ahead]:
        states[i] = begin(i)
        for n in range(i + 1):
            scores(i, n, states[i])
    for pos, i in enumerate(order):
        j = order[pos + ahead] if pos + ahead < n_blocks else None
        n_scores = 0
        if j is not None:
            states[j] = begin(j)
            n_scores = j + 1
        for n in range(max(i + 1, n_scores)):
            if n < n_scores:
                scores(j, n, states[j])
            if n <= i:
                weigh(i, n, states[i])
        finish(i, states.pop(i))


def _moba(q, k, vt, bias_tiles, cfar):
    bsz, n_heads, seq, hd = q.shape
    head_major = lambda b, h: (b, h, 0, 0)
    return pl.pallas_call(
        _moba_kernel,
        grid=(bsz, n_heads),
        in_specs=[
            pl.BlockSpec((1, 1, seq, hd), head_major),
            pl.BlockSpec((1, 1, seq, hd), head_major),
            pl.BlockSpec((1, 1, VT_ROWS, seq), head_major),
            pl.BlockSpec((1, 2, MOBA_BLOCK, MOBA_BLOCK), lambda b, h: (h, 0, 0, 0)),
            pl.BlockSpec((1, 1, MOBA_BLOCK), lambda b, h: (h, 0, 0)),
        ],
        out_specs=pl.BlockSpec((1, seq, hd), lambda b, h: (b, 0, h)),
        out_shape=jax.ShapeDtypeStruct((bsz, seq, n_heads * hd), BF16),
        compiler_params=pltpu.CompilerParams(
            dimension_semantics=("parallel", "parallel"), vmem_limit_bytes=VMEM_LIMIT_BYTES),
        name="moba",
    )(q, k, vt, bias_tiles, cfar)


def _rel_bucket_table():
    n = np.arange(2 * MOBA_BLOCK, dtype=np.int32)
    max_exact = REL_BUCKETS // 2
    nf = np.maximum(n, 1).astype(np.float32)
    ratio = np.log(nf / np.float32(max_exact)) / np.float32(math.log(REL_MAX_DIST / max_exact))
    large = max_exact + (ratio * np.float32(REL_BUCKETS - max_exact)).astype(np.int32)
    large = np.minimum(large, REL_BUCKETS - 1)
    return np.where(n < max_exact, n, large).astype(np.int32)


def _rel_bias_tiles(rel_table):
    bucket = _rel_bucket_table()
    assert np.all(np.diff(bucket) >= 0) and int(bucket[MOBA_BLOCK + 1]) == REL_BUCKETS - 1
    key = np.arange(MOBA_BLOCK)[:, None]
    qry = np.arange(MOBA_BLOCK)[None, :]
    dist = jnp.asarray(np.stack([qry - key, qry - key + MOBA_BLOCK]))[None]
    table_t = (rel_table.T.astype(F32) * LOG2E)[:, :, None, None, None]
    tiles = jnp.broadcast_to(table_t[:, 0], (MOBA_HEADS, 2, MOBA_BLOCK, MOBA_BLOCK))
    for d in range(1, 2 * MOBA_BLOCK):
        if bucket[d] != bucket[d - 1]:
            tiles = jnp.where(dist >= d, table_t[:, int(bucket[d])], tiles)
    tiles = jnp.where(dist >= 0, tiles, NEG_INF)
    cfar = jnp.broadcast_to(table_t[:, REL_BUCKETS - 1, 0], (MOBA_HEADS, 1, MOBA_BLOCK))
    return tiles, cfar


def _merge_ffn_kernel(h_ref, pa_ref, yb_ref, wgb_ref, wpb_ref, wout_ref, g2_ref, b2_ref,
                      wg_ref, wu_ref, wd_ref, g3_ref, b3_ref, o_ref):
    groups = _row_groups(h_ref.shape[0])
    hs = [h_ref[rs, :] for rs in groups]
    mixed = []
    for rs, h in zip(groups, hs):
        gate_b = jax.nn.sigmoid(_dot(h.astype(BF16), wgb_ref[...]))
        merged = pa_ref[rs, :].astype(F32) + gate_b * _dot(yb_ref[rs, :], wpb_ref[...])
        mixed.append(_dot(merged.astype(BF16), wout_ref[...]))
    h2s = [_layer_norm(DEEPNORM_ALPHA * h + m, g2_ref[...], b2_ref[...]) for h, m in zip(hs, mixed)]
    fs = [_swiglu(h2.astype(BF16), wg_ref, wu_ref, wd_ref) for h2 in h2s]
    for rs, h2, f in zip(groups, h2s, fs):
        o_ref[rs, :] = _layer_norm(DEEPNORM_ALPHA * h2 + MACARON_WEIGHT * f, g3_ref[...], b3_ref[...])


def _merge_ffn(h1, pa, yb, w_gb, w_pb, w_out, g2, b2, wg, wu, wd, g3, b3):
    n_tok = h1.shape[0]
    tm = MERGE_TOKEN_TILE
    tok = lambda: pl.BlockSpec((tm, D_MODEL), lambda t: (t, 0))
    wspec = _resident((D_MODEL, D_MODEL))
    vec = _resident((1, D_MODEL))
    return pl.pallas_call(
        _merge_ffn_kernel,
        grid=(n_tok // tm,),
        in_specs=[tok(), tok(), tok(), wspec, wspec, wspec, vec, vec,
                  _resident((D_MODEL, D_FF)), _resident((D_MODEL, D_FF)), _resident((D_FF, D_MODEL)),
                  vec, vec],
        out_specs=tok(),
        out_shape=jax.ShapeDtypeStruct((n_tok, D_MODEL), F32),
        compiler_params=pltpu.CompilerParams(
            dimension_semantics=("parallel",), vmem_limit_bytes=VMEM_LIMIT_BYTES),
        name="merge_ffn",
    )(h1, pa, yb, w_gb, w_pb, w_out, g2, b2, wg, wu, wd, g3, b3)


def kernel(x, ffn1_w_gate, ffn1_w_up, ffn1_w_down, ln1_g, ln1_b, w_in, sg_ln_g, sg_ln_b, sg_w_s, sg_b_s,
           rel_table, w_proj_a, w_proj_b, w_out, ln2_g, ln2_b, ffn2_w_gate, ffn2_w_up, ffn2_w_down,
           ln3_g, ln3_b):
    bsz, seq, d = x.shape
    n_tok = bsz * seq
    bias_tiles, cfar = _rel_bias_tiles(rel_table)
    vec = lambda a: a.reshape(1, -1).astype(F32)
    h = x.reshape(n_tok, d)
    for l in range(DEPTH):
        h = _ffn_ln(h, ffn1_w_gate[l].astype(BF16), ffn1_w_up[l].astype(BF16),
                    ffn1_w_down[l].astype(BF16), vec(ln1_g[l]), vec(ln1_b[l]))
        wi = w_in[l].astype(BF16)
        cols = [wi[:, c * D_MODEL:(c + 1) * D_MODEL] for c in range(7)]
        b_s_bcast = jnp.broadcast_to(sg_b_s[l][:, :, None], (SG_GROUPS, SG_CHUNK, SG_CHUNK)).astype(F32)
        pa, q, k, vt = _mixer_in(h.reshape(bsz, seq, d), cols[0], cols[1], cols[2], cols[3], cols[4],
                                 cols[5], vec(sg_ln_g[l]), vec(sg_ln_b[l]), sg_w_s[l].astype(F32),
                                 b_s_bcast, w_proj_a[l].astype(BF16))
        yb = _moba(q, k, vt, bias_tiles, cfar)
        h = _merge_ffn(h, pa.reshape(n_tok, d), yb.reshape(n_tok, d), cols[6], w_proj_b[l].astype(BF16),
                       w_out[l].astype(BF16), vec(ln2_g[l]), vec(ln2_b[l]),
                       ffn2_w_gate[l].astype(BF16), ffn2_w_up[l].astype(BF16),
                       ffn2_w_down[l].astype(BF16), vec(ln3_g[l]), vec(ln3_b[l]))
    return h.reshape(bsz, seq, d)
```
